```python
import math
import jax
import jax.numpy as jnp
from jax import lax
import numpy as np

D_MODEL = 1024
BATCH = 4
SEQ = 8192
DEPTH = 2

GRID_W = 64
CTX_LEN = 256

D_MIX = D_MODEL
BRANCH_W = D_MIX // 4

MLA_HEADS = 4
MLA_NOPE = 64
MLA_ROPE = 32
MLA_V = BRANCH_W // MLA_HEADS
Q_LORA = 192
KV_LORA = 128
ROPE_BASE = 10000.0
Q_BLOCK = 128

POOL_WINDOWS = (2, 4, 8, 16)
N_POOL = 4
POOL_GC = BRANCH_W // N_POOL

S5_H = 16
S5_G = BRANCH_W // S5_H
S5_P = 64
DT_MIN = 1e-3
DT_MAX = 1e-1

SGU_HEADS = 4
SGU_HD = BRANCH_W // SGU_HEADS
CHUNK = 128

IN_SPLITS = (KV_LORA,
             KV_LORA + MLA_ROPE,
             KV_LORA + MLA_ROPE + BRANCH_W,
             KV_LORA + MLA_ROPE + BRANCH_W + Q_LORA,
             KV_LORA + MLA_ROPE + 2 * BRANCH_W + Q_LORA,
             KV_LORA + MLA_ROPE + 3 * BRANCH_W + Q_LORA,
             KV_LORA + MLA_ROPE + 4 * BRANCH_W + Q_LORA)
CTX_STATE_COLS = KV_LORA + MLA_ROPE + BRANCH_W
IN_DIM = KV_LORA + MLA_ROPE + 4 * BRANCH_W + Q_LORA + D_MIX

ALPHA = (2 * DEPTH) ** 0.25
BETA = (8 * DEPTH) ** -0.25
LN_EPS = 1e-6

kernel_name = 'hybrid_mla_pool_s5_sgu_flow_block'


def _ln(x):
    xf = x.astype(jnp.float32)
    xc = xf - jnp.mean(xf, -1, keepdims=True)
    return xc * lax.rsqrt(jnp.mean(xc * xc, -1, keepdims=True) + LN_EPS)


def _rms(x, g):
    xf = x.astype(jnp.float32)
    return xf * lax.rsqrt(jnp.mean(xf * xf, -1, keepdims=True) + LN_EPS) * g


def _axial_rope_tables(rows):
    row = jnp.repeat(jnp.arange(rows, dtype=jnp.float32), GRID_W)
    col = jnp.tile(jnp.arange(GRID_W, dtype=jnp.float32), rows)
    nf = MLA_ROPE // 4
    inv = ROPE_BASE ** (-jnp.arange(nf, dtype=jnp.float32) / nf)
    ang = jnp.stack([row[:, None] * inv, col[:, None] * inv], axis=1)
    return jnp.cos(ang), jnp.sin(ang)


def _rope(x, cos, sin):
    shp = x.shape
    xr = x.reshape(shp[:-1] + (2, 2, MLA_ROPE // 4))
    x1, x2 = xr[..., 0, :], xr[..., 1, :]
    cs, sn = cos[:, None], sin[:, None]
    out = jnp.stack([x1 * cs - x2 * sn, x2 * cs + x1 * sn], axis=-2)
    return out.reshape(shp)


def _mla_q(cq, g_q, w_uq):
    b, n = cq.shape[:2]
    q = (_rms(cq, g_q) @ w_uq).reshape(b, n, MLA_HEADS, MLA_NOPE + MLA_ROPE)
    return q[..., :MLA_NOPE], q[..., MLA_NOPE:]


def _mla_kv(ckv, g_kv, w_ukv):
    b, n = ckv.shape[:2]
    kv = (_rms(ckv, g_kv) @ w_ukv).reshape(b, n, MLA_HEADS, MLA_NOPE + MLA_V)
    return kv[..., :MLA_NOPE], kv[..., MLA_NOPE:]


def _mla_attend(q_nope, q_rope, k_nope, k_rope, v):
    b, n = q_nope.shape[:2]
    nb = n // Q_BLOCK
    scale = (MLA_NOPE + MLA_ROPE) ** -0.5
    vf = v.astype(jnp.float32)

    def blocks(t):
        return t.reshape((b, nb, Q_BLOCK) + t.shape[2:]).swapaxes(0, 1)

    def one(qs):
        qn, qr = qs
        s = (jnp.einsum('bqhd,bkhd->bhqk', qn, k_nope)
             + jnp.einsum('bqhr,bkr->bhqk', qr, k_rope))
        p = jax.nn.softmax(s.astype(jnp.float32) * scale, axis=-1)
        return jnp.einsum('bhqk,bkhd->bqhd', p, vf)

    o = lax.map(one, (blocks(q_nope), blocks(q_rope)))
    return o.swapaxes(0, 1).reshape(b, n, MLA_HEADS * MLA_V)


def _pool_mixer(vin, w_pool, scale):
    b, n, _ = vin.shape
    vf = vin.astype(jnp.float32)
    cs = jnp.concatenate([jnp.zeros((b, 1, BRANCH_W), jnp.float32), jnp.cumsum(vf, axis=1)], axis=1)
    t = jnp.arange(n)
    means = []
    for gi, w in enumerate(POOL_WINDOWS):
        lo = jnp.clip(t - w // 2, 0, n)
        hi = jnp.clip(t + (w - w // 2), 0, n)
        csg = cs[..., gi * POOL_GC:(gi + 1) * POOL_GC]
        tot = jnp.take(csg, hi, axis=1) - jnp.take(csg, lo, axis=1)
        means.append(tot / (hi - lo).astype(jnp.float32)[:, None])
    pooled = jnp.concatenate(means, -1) - vf
    y = jnp.einsum('bngc,gcd->bngd', pooled.reshape(b, n, N_POOL, POOL_GC), w_pool)
    return y.reshape(b, n, BRANCH_W) * scale


def _s5_discretise(lam_re, lam_im, log_dt, b_re, b_im):
    lam_re = lam_re.astype(jnp.float32)
    lam_im = lam_im.astype(jnp.float32)
    dt = jnp.exp(log_dt.astype(jnp.float32))[:, None]
    mag = jnp.exp(lam_re * dt)
    lb_re, lb_im = mag * jnp.cos(lam_im * dt), mag * jnp.sin(lam_im * dt)
    nr, ni = lb_re - 1.0, lb_im
    den = lam_re * lam_re + lam_im * lam_im
    k_re = (nr * lam_re + ni * lam_im) / den
    k_im = (ni * lam_re - nr * lam_im) / den
    bb_re = k_re[..., None] * b_re - k_im[..., None] * b_im
    bb_im = k_re[..., None] * b_im + k_im[..., None] * b_re
    return lb_re, lb_im, bb_re, bb_im


def _cscan_combine(e1, e2):
    a1r, a1i, b1r, b1i = e1
    a2r, a2i, b2r, b2i = e2
    return (a2r * a1r - a2i * a1i, a2r * a1i + a2i * a1r,
            a2r * b1r - a2i * b1i + b2r, a2r * b1i + a2i * b1r + b2i)


def _s5_scan(u, disc, reverse, h0):
    lb_re, lb_im, bb_re, bb_im = disc
    n = u.shape[1]
    bu_re = jnp.einsum('bngh,gph->nbgp', u, bb_re)
    bu_im = jnp.einsum('bngh,gph->nbgp', u, bb_im)
    if h0 is not None:
        i0 = n - 1 if reverse else 0
        h0r, h0i = h0
        bu_re = bu_re.at[i0].add(lb_re * h0r - lb_im * h0i)
        bu_im = bu_im.at[i0].add(lb_re * h0i + lb_im * h0r)
    a_re = jnp.broadcast_to(lb_re, (n, 1) + lb_re.shape)
    a_im = jnp.broadcast_to(lb_im, (n, 1) + lb_im.shape)
    _, _, hr, hi = lax.associative_scan(_cscan_combine, (a_re, a_im, bu_re, bu_im),
                                        reverse=reverse, axis=0)
    return hr, hi


def _s5_readout(hr, hi, c_re, c_im):
    return jnp.einsum('nbgp,ghp->bngh', hr, c_re) - jnp.einsum('nbgp,ghp->bngh', hi, c_im)


def _glu(y, w_glu, b_glu):
    g = jax.nn.gelu(y)
    return g * jax.nn.sigmoid(g @ w_glu + b_glu)


def _sgu(u, v, g, bta, w_s, b_s):
    b, n, _ = v.shape
    vn = _ln(v) * g + bta
    vc = vn.reshape(b, n // CHUNK, CHUNK, SGU_HEADS, SGU_HD)
    mixed = jnp.einsum('hts,bcshd->bcthd', w_s, vc) + b_s.T[:, :, None]
    return u * mixed.reshape(b, n, BRANCH_W)


def _layer(x, ctx, c, c_ctx, cos, sin, p, last):
    b, n, _ = x.shape
    nc = ctx.shape[1]
    mod = jax.nn.silu(c.astype(jnp.float32)) @ p['w_mod'] + p['b_mod']
    shift, scale, gate = jnp.split(mod[:, None, :], 3, axis=-1)
    h = _ln(x) * (1.0 + scale) + shift
    ckv, kr, s5_in, cq, pool_in, su, sv, g = jnp.split(h @ p['w_in'], IN_SPLITS, axis=-1)

    sc = jax.nn.silu(c_ctx.astype(jnp.float32))
    if last:
        mod_c = sc @ p['w_mod'][:, :2 * D_MODEL] + p['b_mod'][:2 * D_MODEL]
        shift_c, scale_c = jnp.split(mod_c, 2)
        hc = _ln(ctx) * (1.0 + scale_c) + shift_c
        ckv_c, kr_c, s5_c = jnp.split(hc @ p['w_in'][:, :CTX_STATE_COLS], IN_SPLITS[:2], axis=-1)
    else:
        mod_c = sc @ p['w_mod'] + p['b_mod']
        shift_c, scale_c, gate_c = jnp.split(mod_c, 3)
        hc = _ln(ctx) * (1.0 + scale_c) + shift_c
        ckv_c, kr_c, s5_c, cq_c, pool_c, su_c, sv_c, g_c = jnp.split(hc @ p['w_in'], IN_SPLITS, axis=-1)

    kn_c, v_c = _mla_kv(ckv_c, p['g_kv'], p['w_ukv'])
    kn_l, v_l = _mla_kv(ckv, p['g_kv'], p['w_ukv'])
    qn_l, qr_l = _mla_q(cq, p['g_q'], p['w_uq'])
    qr_l = _rope(qr_l, cos, sin)
    kr_l = _rope(kr[:, :, None, :], cos, sin)[:, :, 0, :]
    att_l = _mla_attend(qn_l, qr_l,
                        jnp.concatenate([kn_c, kn_l], axis=1),
                        jnp.concatenate([kr_c, kr_l], axis=1),
                        jnp.concatenate([v_c, v_l], axis=1))

    pool_l = _pool_mixer(pool_in, p['w_pool'], p['pool_scale'])

    u_l = s5_in.astype(jnp.float32).reshape(b, n, S5_G, S5_H)
    u_c = s5_c.astype(jnp.float32).reshape(b, nc, S5_G, S5_H)
    y_l = s5_in * p['s5_d']
    if not last:
        y_c = s5_c * p['s5_d']
    for d, rev in enumerate((False, True)):
        disc = _s5_discretise(p['lam_re'][d], p['lam_im'][d], p['log_dt'][d],
                              p['s5_b_re'][d], p['s5_b_im'][d])
        hc_r, hc_i = _s5_scan(u_c, disc, rev, None)
        i_end = 0 if rev else nc - 1
        hl_r, hl_i = _s5_scan(u_l, disc, rev, (hc_r[i_end], hc_i[i_end]))
        y_l = y_l + _s5_readout(hl_r, hl_i, p['s5_c_re'][d], p['s5_c_im'][d]).reshape(b, n, BRANCH_W)
        if not last:
            y_c = y_c + _s5_readout(hc_r, hc_i, p['s5_c_re'][d], p['s5_c_im'][d]).reshape(b, nc, BRANCH_W)
    s5_l = _glu(y_l, p['w_glu'], p['b_glu'])

    sgu_l = _sgu(su, sv, p['sgu_g'], p['sgu_b'], p['w_s'], p['b_s'])

    y = (jnp.concatenate([att_l, pool_l, s5_l, sgu_l], axis=-1) * jax.nn.silu(g)) @ p['w_out']
    x_new = (_ln(ALPHA * x + gate * y) * p['ln_g'] + p['ln_b']).astype(x.dtype)
    if last:
        return x_new, None

    qn_c, qr_c = _mla_q(cq_c, p['g_q'], p['w_uq'])
    att_c = _mla_attend(qn_c, qr_c, kn_c, kr_c, v_c)
    pool_cc = _pool_mixer(pool_c, p['w_pool'], p['pool_scale'])
    s5_cc = _glu(y_c, p['w_glu'], p['b_glu'])
    sgu_c = _sgu(su_c, sv_c, p['sgu_g'], p['sgu_b'], p['w_s'], p['b_s'])
    yc = (jnp.concatenate([att_c, pool_cc, s5_cc, sgu_c], axis=-1) * jax.nn.silu(g_c)) @ p['w_out']
    ctx_new = (_ln(ALPHA * ctx + gate_c * yc) * p['ln_g'] + p['ln_b']).astype(ctx.dtype)
    return x_new, ctx_new


def setup_inputs(seed: int = 0) -> dict:
    key = jax.random.key(seed)
    k = jax.random.split(key, 32)
    L = DEPTH
    f32 = jnp.float32

    def nrm(i, shape, s):
        return jax.random.normal(k[i], shape, f32) * s

    n_idx = jnp.arange(S5_P, dtype=f32)
    return {
        'x': nrm(0, (BATCH, SEQ, D_MODEL), 1.0),
        'c': nrm(1, (BATCH, D_MODEL), 1.0),
        'ctx': nrm(2, (BATCH, CTX_LEN, D_MODEL), 1.0),
        'c_ctx': nrm(3, (D_MODEL,), 1.0),
        'w_mod': nrm(4, (L, D_MODEL, 3 * D_MODEL), 0.25 * D_MODEL ** -0.5),
        'b_mod': nrm(5, (L, 3 * D_MODEL), 0.02),
        'w_in': nrm(6, (L, D_MODEL, IN_DIM), D_MODEL ** -0.5),
        'g_q': 1.0 + nrm(7, (L, Q_LORA), 0.02),
        'w_uq': nrm(8, (L, Q_LORA, MLA_HEADS * (MLA_NOPE + MLA_ROPE)), Q_LORA ** -0.5),
        'g_kv': 1.0 + nrm(9, (L, KV_LORA), 0.02),
        'w_ukv': nrm(10, (L, KV_LORA, MLA_HEADS * (MLA_NOPE + MLA_V)), KV_LORA ** -0.5),
        'w_pool': nrm(11, (L, N_POOL, POOL_GC, POOL_GC), POOL_GC ** -0.5),
        'pool_scale': 1.0 + nrm(12, (L, BRANCH_W), 0.02),
        'lam_re': -0.5 + nrm(13, (L, 2, S5_G, S5_P), 0.01),
        'lam_im': math.pi * n_idx + nrm(14, (L, 2, S5_G, S5_P), 0.01),
        'log_dt': jax.random.uniform(k[15], (L, 2, S5_G), dtype=f32,
                                     minval=math.log(DT_MIN), maxval=math.log(DT_MAX)),
        's5_b_re': nrm(16, (L, 2, S5_G, S5_P, S5_H), (2 * S5_H) ** -0.5),
        's5_b_im': nrm(17, (L, 2, S5_G, S5_P, S5_H), (2 * S5_H) ** -0.5),
        's5_c_re': nrm(18, (L, 2, S5_G, S5_H, S5_P), S5_P ** -0.5),
        's5_c_im': nrm(19, (L, 2, S5_G, S5_H, S5_P), S5_P ** -0.5),
        's5_d': nrm(20, (L, BRANCH_W), 1.0),
        'w_glu': nrm(21, (L, BRANCH_W, BRANCH_W), BRANCH_W ** -0.5),
        'b_glu': nrm(22, (L, BRANCH_W), 0.02),
        'sgu_g': 1.0 + nrm(23, (L, BRANCH_W), 0.02),
        'sgu_b': nrm(24, (L, BRANCH_W), 0.02),
        'w_s': nrm(25, (L, SGU_HEADS, CHUNK, CHUNK), CHUNK ** -0.5),
        'b_s': 1.0 + nrm(26, (L, SGU_HEADS, CHUNK), 0.02),
        'w_out': nrm(27, (L, D_MIX, D_MODEL), BETA * D_MIX ** -0.5),
        'ln_g': 1.0 + nrm(28, (L, D_MODEL), 0.02),
        'ln_b': nrm(29, (L, D_MODEL), 0.02),
    }


def reference(x, c, ctx, c_ctx, w_mod, b_mod, w_in, g_q, w_uq, g_kv, w_ukv, w_pool, pool_scale,
              lam_re, lam_im, log_dt, s5_b_re, s5_b_im, s5_c_re, s5_c_im, s5_d, w_glu, b_glu,
              sgu_g, sgu_b, w_s, b_s, w_out, ln_g, ln_b):
    rows = x.shape[1] // GRID_W
    cos, sin = _axial_rope_tables(rows)
    for l in range(DEPTH):
        p = {
            'w_mod': w_mod[l], 'b_mod': b_mod[l], 'w_in': w_in[l],
            'g_q': g_q[l], 'w_uq': w_uq[l], 'g_kv': g_kv[l], 'w_ukv': w_ukv[l],
            'w_pool': w_pool[l], 'pool_scale': pool_scale[l],
            'lam_re': lam_re[l], 'lam_im': lam_im[l], 'log_dt': log_dt[l],
            's5_b_re': s5_b_re[l], 's5_b_im': s5_b_im[l], 's5_c_re': s5_c_re[l], 's5_c_im': s5_c_im[l],
            's5_d': s5_d[l], 'w_glu': w_glu[l], 'b_glu': b_glu[l],
            'sgu_g': sgu_g[l], 'sgu_b': sgu_b[l], 'w_s': w_s[l], 'b_s': b_s[l],
            'w_out': w_out[l], 'ln_g': ln_g[l], 'ln_b': ln_b[l],
        }
        x, ctx = _layer(x, ctx, c, c_ctx, cos, sin, p, l == DEPTH - 1)
    return x
```

```python
import functools
import math

import jax
import jax.numpy as jnp
from jax import lax
from jax.experimental import pallas as pl
from jax.experimental.pallas import tpu as pltpu

F32 = jnp.float32
BF16 = jnp.bfloat16

GRID_W = 64
N_HEADS = 4
NOPE = 64
ROPE = 32
V_DIM = 64
Q_LORA = 192
KV_LORA = 128
ROPE_BASE = 10000.0
BRANCH = 256
POOL_WINDOWS = (2, 4, 8, 16)
POOL_GC = 64
S5_H = 16
S5_G = 16
S5_P = 64
SGU_HEADS = 4
SGU_HD = 64
CHUNK = 128
LN_EPS = 1e-6

LANE = 128
HEAD_PAD = 128
V7X_VMEM_LIMIT = 56 * 1024 * 1024
S5_T = 16
S5_BLK = S5_T * LANE

C_CKV = 0
C_CQ = 128
C_KR = 384
C_S5 = 512
C_MIX = 768
IN_PAD = 2560
MIX_W = 3 * BRANCH + 4 * BRANCH


def _dot(a, b):
    return jnp.dot(a, b, preferred_element_type=F32)


def _dot_nt(a, b):
    return lax.dot_general(a, b, (((1,), (1,)), ((), ())), preferred_element_type=F32)


def _ln_rows(x):
    mu = jnp.mean(x, axis=-1, keepdims=True)
    xc = x - mu
    var = jnp.mean(xc * xc, axis=-1, keepdims=True)
    return xc * lax.rsqrt(var + LN_EPS)


def _sigmoid(x):
    return 1.0 / (1.0 + jnp.exp(-x))


def _mod_kernel(c_ref, w_ref, b_ref, o_ref):
    c = c_ref[...]
    s = c * _sigmoid(c)
    o_ref[0] = jnp.dot(s, w_ref[0], preferred_element_type=F32,
                       precision=lax.Precision.HIGHEST) + b_ref[0]


def _modulation(cc, w_mod, b_mod):
    L, D, D3 = w_mod.shape
    nb = D3 // D
    return pl.pallas_call(
        _mod_kernel,
        out_shape=jax.ShapeDtypeStruct((L, 8, D3), F32),
        grid=(L, nb),
        in_specs=[pl.BlockSpec((8, D), lambda l, j: (0, 0)),
                  pl.BlockSpec((1, D, D), lambda l, j: (l, 0, j)),
                  pl.BlockSpec((1, 1, D), lambda l, j: (l, 0, j))],
        out_specs=pl.BlockSpec((1, 8, D), lambda l, j: (l, 0, j)),
        compiler_params=pltpu.CompilerParams(vmem_limit_bytes=V7X_VMEM_LIMIT),
        name="modulation",
    )(cc, w_mod, b_mod)


def _rope_lanes(x, c, s1, s2):
    w = x.shape[-1]
    return x * c + pltpu.roll(x, w - 8, 1) * s1 + pltpu.roll(x, 8, 1) * s2


def _inproj_kernel(x_ref, shift_ref, scale_ref, w_ref, gkv_ref, gq_ref, wk_ref, wv_ref, wq_ref,
                   pk_ref, vone_ref, rc_ref, rs1_ref, rs2_ref,
                   q_ref, k_ref, v_ref, s5_ref, mix_ref):
    x = x_ref[0]
    h = _ln_rows(x) * (1.0 + scale_ref[0]) + shift_ref[0]
    proj = _dot(h.astype(BF16), w_ref[...])
    ckv = proj[:, C_CKV:C_CKV + KV_LORA]
    cq = proj[:, C_CQ:C_CQ + Q_LORA]
    kr = proj[:, C_KR:C_KR + LANE]

    rk = ckv * lax.rsqrt(jnp.mean(ckv * ckv, axis=-1, keepdims=True) + LN_EPS) * gkv_ref[...]
    rk = rk.astype(BF16)
    rq = cq * lax.rsqrt(jnp.mean(cq * cq, axis=-1, keepdims=True) + LN_EPS) * gq_ref[...]
    rq = rq.astype(BF16)

    rc, rs1, rs2 = rc_ref[...], rs1_ref[...], rs2_ref[...]
    krp = _rope_lanes(_dot(kr.astype(BF16), pk_ref[...]), rc, rs1, rs2)
    kn = _dot(rk, wk_ref[...])
    vv = _dot(rk, wv_ref[...]) + vone_ref[...]
    qq = _dot(rq, wq_ref[...])
    for hd in range(N_HEADS):
        sl = slice(hd * HEAD_PAD, (hd + 1) * HEAD_PAD)
        q_ref[0, hd] = _rope_lanes(qq[:, sl], rc, rs1, rs2).astype(BF16)
        k_ref[0, hd] = (kn[:, sl] + krp).astype(BF16)
        v_ref[0, hd] = vv[:, sl].astype(BF16)
    s5_ref[0] = proj[:, C_S5:C_S5 + BRANCH].astype(BF16)
    mix_ref[0] = proj[:, C_MIX:C_MIX + MIX_W].astype(BF16)


def _inproj(x, shift, scale, batch_mod, wts, tables, tm):
    B, n, D = x.shape
    w_in, gkv, gq, wk, wv, wq, pk, vone = wts
    rc, rs1, rs2 = tables
    mod_map = (lambda b, i: (b, 0, 0)) if batch_mod else (lambda b, i: (0, 0, 0))
    full2 = lambda b, i: (0, 0)
    tab = pl.BlockSpec((tm, LANE), lambda b, i: (i, 0))
    hw = N_HEADS * HEAD_PAD
    return pl.pallas_call(
        _inproj_kernel,
        out_shape=(jax.ShapeDtypeStruct((B, N_HEADS, n, HEAD_PAD), BF16),
                   jax.ShapeDtypeStruct((B, N_HEADS, n, HEAD_PAD), BF16),
                   jax.ShapeDtypeStruct((B, N_HEADS, n, HEAD_PAD), BF16),
                   jax.ShapeDtypeStruct((B, n, BRANCH), BF16),
                   jax.ShapeDtypeStruct((B, n, MIX_W), BF16)),
        grid=(B, n // tm),
        in_specs=[pl.BlockSpec((1, tm, D), lambda b, i: (b, i, 0)),
                  pl.BlockSpec((1, 1, D), mod_map),
                  pl.BlockSpec((1, 1, D), mod_map),
                  pl.BlockSpec((D, IN_PAD), full2),
                  pl.BlockSpec((1, KV_LORA), full2),
                  pl.BlockSpec((1, Q_LORA), full2),
                  pl.BlockSpec((KV_LORA, hw), full2),
                  pl.BlockSpec((KV_LORA, hw), full2),
                  pl.BlockSpec((Q_LORA, hw), full2),
                  pl.BlockSpec((LANE, LANE), full2),
                  pl.BlockSpec((1, hw), full2),
                  tab, tab, tab],
        out_specs=(pl.BlockSpec((1, N_HEADS, tm, HEAD_PAD), lambda b, i: (b, 0, i, 0)),
                   pl.BlockSpec((1, N_HEADS, tm, HEAD_PAD), lambda b, i: (b, 0, i, 0)),
                   pl.BlockSpec((1, N_HEADS, tm, HEAD_PAD), lambda b, i: (b, 0, i, 0)),
                   pl.BlockSpec((1, tm, BRANCH), lambda b, i: (b, i, 0)),
                   pl.BlockSpec((1, tm, MIX_W), lambda b, i: (b, i, 0))),
        compiler_params=pltpu.CompilerParams(
            dimension_semantics=("parallel", "parallel"), vmem_limit_bytes=V7X_VMEM_LIMIT),
        name="inproj",
    )(x, shift, scale, w_in, gkv, gq, wk, wv, wq, pk, vone, rc, rs1, rs2)


def _attn_kernel(*refs, n_src, tks):
    q_ref = refs[0]
    k_refs = refs[1:1 + n_src]
    v_refs = refs[1 + n_src:1 + 2 * n_src]
    o_ref = refs[1 + 2 * n_src]
    s_scrs = refs[2 + 2 * n_src:]
    tq = q_ref.shape[2]
    outs = []
    for hd in range(N_HEADS):
        q = q_ref[0, hd]
        m_acc = jnp.full((tq, LANE), -jnp.inf, F32)
        for si in range(n_src):
            k_ref, s_scr, tk = k_refs[si], s_scrs[si], tks[si]
            nt = k_ref.shape[2] // tk

            def qk_body(j, m, k_ref=k_ref, s_scr=s_scr, tk=tk, q=q, hd=hd):
                kt = k_ref[0, hd, pl.ds(pl.multiple_of(j * tk, tk), tk), :]
                s = _dot_nt(q, kt)
                s_scr[j] = s
                for c in range(tk // LANE):
                    m = jnp.maximum(m, s[:, c * LANE:(c + 1) * LANE])
                return m

            m_acc = lax.fori_loop(0, nt, qk_body, m_acc)
        m_row = jnp.max(m_acc, axis=-1, keepdims=True)
        acc = jnp.zeros((tq, HEAD_PAD), F32)
        for si in range(n_src):
            v_ref, s_scr, tk = v_refs[si], s_scrs[si], tks[si]
            nt = v_ref.shape[2] // tk

            def pv_body(j, a, v_ref=v_ref, s_scr=s_scr, tk=tk, m_row=m_row, hd=hd):
                p = jnp.exp(s_scr[j] - m_row).astype(BF16)
                vt = v_ref[0, hd, pl.ds(pl.multiple_of(j * tk, tk), tk), :]
                return a + _dot(p, vt)

            acc = lax.fori_loop(0, nt, pv_body, acc)
        outs.append(acc[:, :V_DIM] / acc[:, V_DIM:V_DIM + 1])
    o_ref[0] = jnp.concatenate(outs, axis=-1).astype(o_ref.dtype)


def _attention(q, ks, vs, tq):
    B, H, n, _ = q.shape
    n_src = len(ks)
    tks = tuple(min(512, k.shape[2]) for k in ks)
    kv_spec = lambda a: pl.BlockSpec((1, H, a.shape[2], HEAD_PAD), lambda b, i: (b, 0, 0, 0))
    return pl.pallas_call(
        functools.partial(_attn_kernel, n_src=n_src, tks=tks),
        out_shape=jax.ShapeDtypeStruct((B, n, H * V_DIM), BF16),
        grid=(B, n // tq),
        in_specs=[pl.BlockSpec((1, H, tq, HEAD_PAD), lambda b, i: (b, 0, i, 0))]
                 + [kv_spec(k) for k in ks] + [kv_spec(v) for v in vs],
        out_specs=pl.BlockSpec((1, tq, H * V_DIM), lambda b, i: (b, i, 0)),
        scratch_shapes=[pltpu.VMEM((k.shape[2] // tk, tq, tk), F32) for k, tk in zip(ks, tks)],
        compiler_params=pltpu.CompilerParams(
            dimension_semantics=("parallel", "parallel"), vmem_limit_bytes=V7X_VMEM_LIMIT),
        name="attention",
    )(q, *ks, *vs)


def _s5_kernel(ul_ref, uc_ref, m_ref, pb_ref, pc_ref, are_ref, aim_ref, yl_ref, yc_ref,
               ut_ref, sre_ref, sim_ref, f_ref):
    n = ul_ref.shape[1]
    nc = uc_ref.shape[1]
    kl = n // S5_T
    kc = nc // S5_T
    nblk = n // S5_BLK
    rows_blk = S5_BLK // S5_T
    ncol = ut_ref.shape[2]

    half_g = S5_G // 2

    def relayout_in(nrows, col0):
        for s in range(S5_T):
            for h in range(2):
                blk = f_ref[h, pl.ds(s, nrows, stride=S5_T), :]
                if nrows < LANE:
                    blk = jnp.concatenate([blk, jnp.zeros((LANE - nrows, LANE), F32)], axis=0)
                bt = blk.T
                for gg in range(half_g):
                    ut_ref[h * half_g + gg, s * S5_H:(s + 1) * S5_H, col0:col0 + LANE] = (
                        bt[gg * S5_H:(gg + 1) * S5_H, :].astype(BF16))

    for jb in range(nblk):
        for h in range(2):
            f_ref[h] = ul_ref[0, jb * S5_BLK:(jb + 1) * S5_BLK, h * LANE:(h + 1) * LANE].astype(F32)
        relayout_in(rows_blk, jb * LANE)
    for h in range(2):
        f_ref[h, 0:nc, :] = uc_ref[0, :, h * LANE:(h + 1) * LANE].astype(F32)
    relayout_in(kc, kl)

    for g in range(S5_G):
        st = _dot(pb_ref[g], ut_ref[g])
        stt = st.T
        sre_ref[:, g * LANE:(g + 1) * LANE] = stt[:, :LANE]
        sim_ref[:, g * LANE:(g + 1) * LANE] = stt[:, LANE:]

    are = are_ref[...]
    aim = aim_ref[...]
    lane = lax.broadcasted_iota(jnp.int32, (1, S5_G * LANE), 1)
    is_f = (lane % LANE) < S5_P

    def step(rf, rb, carry):
        hr, hi = carry
        srf, sif = sre_ref[pl.ds(rf, 1), :], sim_ref[pl.ds(rf, 1), :]
        srb, sib = sre_ref[pl.ds(rb, 1), :], sim_ref[pl.ds(rb, 1), :]
        sre_ref[pl.ds(rf, 1), :] = jnp.where(is_f, hr, srf)
        sim_ref[pl.ds(rf, 1), :] = jnp.where(is_f, hi, sif)
        sre_ref[pl.ds(rb, 1), :] = jnp.where(is_f, srb, hr)
        sim_ref[pl.ds(rb, 1), :] = jnp.where(is_f, sib, hi)
        sr = jnp.where(is_f, srf, srb)
        si = jnp.where(is_f, sif, sib)
        return (are * hr - aim * hi + sr, are * hi + aim * hr + si)

    zero = jnp.zeros((1, S5_G * LANE), F32)
    carry = lax.fori_loop(0, kc, lambda i, c: step(kl + i, kl + kc - 1 - i, c), (zero, zero))
    lax.fori_loop(0, kl, lambda i, c: step(i, kl - 1 - i, c), carry)

    for g in range(S5_G):
        hin = jnp.concatenate([sre_ref[:, g * LANE:(g + 1) * LANE],
                               sim_ref[:, g * LANE:(g + 1) * LANE]], axis=1)
        hin_t = hin.T.astype(BF16)
        y = _dot(m_ref[g], ut_ref[g]) + _dot(pc_ref[g], hin_t)
        ut_ref[g] = y.astype(BF16)

    def relayout_out(nrows, col0):
        for t in range(S5_T):
            for h in range(2):
                yt = jnp.concatenate(
                    [ut_ref[h * half_g + gg, t * S5_H:(t + 1) * S5_H, col0:col0 + LANE]
                     for gg in range(half_g)], axis=0).astype(F32)
                f_ref[h, pl.ds(t, nrows, stride=S5_T), :] = yt.T[:nrows]

    for jb in range(nblk):
        relayout_out(rows_blk, jb * LANE)
        for h in range(2):
            yl_ref[0, jb * S5_BLK:(jb + 1) * S5_BLK, h * LANE:(h + 1) * LANE] = (
                f_ref[h].astype(yl_ref.dtype))
    relayout_out(kc, kl)
    for h in range(2):
        yc_ref[0, :, h * LANE:(h + 1) * LANE] = f_ref[h, 0:nc, :].astype(yc_ref.dtype)


def _s5(ul, uc, ops):
    B, n, _ = ul.shape
    nc = uc.shape[1]
    m, pb, pc, are, aim = ops
    assert n % S5_BLK == 0 and nc % S5_T == 0 and nc // S5_T <= LANE and nc <= S5_BLK
    assert (n // S5_T) % 2 == 0 and (nc // S5_T) % 2 == 0
    ncol = n // S5_T + LANE
    w3 = lambda b: (0, 0, 0)
    return pl.pallas_call(
        _s5_kernel,
        out_shape=(jax.ShapeDtypeStruct((B, n, BRANCH), BF16),
                   jax.ShapeDtypeStruct((B, nc, BRANCH), BF16)),
        grid=(B,),
        in_specs=[pl.BlockSpec((1, n, BRANCH), lambda b: (b, 0, 0)),
                  pl.BlockSpec((1, nc, BRANCH), lambda b: (b, 0, 0)),
                  pl.BlockSpec(m.shape, w3), pl.BlockSpec(pb.shape, w3), pl.BlockSpec(pc.shape, w3),
                  pl.BlockSpec(are.shape, lambda b: (0, 0)), pl.BlockSpec(aim.shape, lambda b: (0, 0))],
        out_specs=(pl.BlockSpec((1, n, BRANCH), lambda b: (b, 0, 0)),
                   pl.BlockSpec((1, nc, BRANCH), lambda b: (b, 0, 0))),
        scratch_shapes=[pltpu.VMEM((S5_G, S5_T * S5_H, ncol), BF16),
                        pltpu.VMEM((ncol, S5_G * LANE), F32),
                        pltpu.VMEM((ncol, S5_G * LANE), F32),
                        pltpu.VMEM((2, S5_BLK, LANE), F32)],
        compiler_params=pltpu.CompilerParams(
            dimension_semantics=("parallel",), vmem_limit_bytes=V7X_VMEM_LIMIT),
        name="s5",
    )(ul, uc, m, pb, pc, are, aim)


POOL_HALO = 16


def _gelu_tanh(x):
    return 0.5 * x * (1.0 + jnp.tanh(math.sqrt(2.0 / math.pi) * (x + 0.044715 * (x * x * x))))


def _merge_kernel(x_ref, att_ref, y5_ref, mix_ref, prev_ref, next_ref, gate_ref,
                  wpool_ref, pscale_ref, wglu_ref, bglu_ref, sg_ref, sb_ref, ws_ref, bs_ref,
                  wout_ref, lng_ref, lnb_ref, o_ref, a_ref, b2_ref, b4_ref, b8_ref,
                  *, n_total, alpha):
    tm = x_ref.shape[1]
    i = pl.program_id(1)
    ni = pl.num_programs(1)
    hl = POOL_HALO

    pool_in = mix_ref[0, :, 0:BRANCH].astype(F32)
    a_ref[0:hl, :] = jnp.where(i > 0, prev_ref[0].astype(F32), 0.0)
    a_ref[hl:hl + tm, :] = pool_in
    a_ref[hl + tm:hl + tm + hl, :] = jnp.where(i < ni - 1, next_ref[0].astype(F32), 0.0)
    r2 = tm + 16
    b2_ref[8:8 + r2, :] = a_ref[7:7 + r2, :] + a_ref[8:8 + r2, :]
    r4 = tm + 12
    b4_ref[10:10 + r4, :] = b2_ref[9:9 + r4, :] + b2_ref[11:11 + r4, :]
    r8 = tm + 8
    b8_ref[12:12 + r8, :] = b4_ref[10:10 + r8, :] + b4_ref[14:14 + r8, :]
    c2 = b2_ref[hl:hl + tm, :]
    c4 = b4_ref[hl:hl + tm, :]
    c8 = b8_ref[hl:hl + tm, :]
    c16 = b8_ref[hl - 4:hl - 4 + tm, :] + b8_ref[hl + 4:hl + 4 + tm, :]
    lane = lax.broadcasted_iota(jnp.int32, (tm, BRANCH), 1)
    grp = lane // POOL_GC
    tot = jnp.where(grp == 0, c2, jnp.where(grp == 1, c4, jnp.where(grp == 2, c8, c16)))
    t = lax.broadcasted_iota(jnp.int32, (tm, BRANCH), 0) + i * tm
    half = jnp.left_shift(1, grp)
    cnt = jnp.minimum(t + half, n_total) - jnp.maximum(t - half, 0)
    pooled = tot / cnt.astype(F32) - pool_in
    pool_l = _dot(pooled.astype(BF16), wpool_ref[...]) * pscale_ref[...]

    g1 = _gelu_tanh(y5_ref[0].astype(F32))
    s5_l = g1 * _sigmoid(_dot(g1.astype(BF16), wglu_ref[...]) + bglu_ref[...])

    su = mix_ref[0, :, BRANCH:2 * BRANCH].astype(F32)
    sv = mix_ref[0, :, 2 * BRANCH:3 * BRANCH].astype(F32)
    vn = (_ln_rows(sv) * sg_ref[...] + sb_ref[...]).astype(BF16)
    lane_c = lax.broadcasted_iota(jnp.int32, (CHUNK, BRANCH), 1) // SGU_HD
    mixed = []
    for c in range(tm // CHUNK):
        vc = vn[c * CHUNK:(c + 1) * CHUNK, :]
        r = _dot(ws_ref[SGU_HEADS - 1], vc)
        for hd in range(SGU_HEADS - 2, -1, -1):
            r = jnp.where(lane_c == hd, _dot(ws_ref[hd], vc), r)
        mixed.append(r + bs_ref[...])
    sgu_l = su * jnp.concatenate(mixed, axis=0)

    gts = mix_ref[0, :, 3 * BRANCH:].astype(F32)
    cat = jnp.concatenate([att_ref[0].astype(F32), pool_l, s5_l, sgu_l], axis=-1)
    y = _dot((cat * (gts * _sigmoid(gts))).astype(BF16), wout_ref[...])
    z = alpha * x_ref[0] + gate_ref[0] * y
    o_ref[0] = _ln_rows(z) * lng_ref[...] + lnb_ref[...]


def _merge(x, att, y5, mix, gate, batch_mod, wts, tm, alpha):
    B, n, D = x.shape
    wpool, pscale, wglu, bglu, sg, sb, ws, bs, wout, lng, lnb = wts
    hb = tm // POOL_HALO
    nhb = n // POOL_HALO
    full2 = lambda b, i: (0, 0)
    mod_map = (lambda b, i: (b, 0, 0)) if batch_mod else (lambda b, i: (0, 0, 0))
    row = lambda w: pl.BlockSpec((1, w), full2)
    return pl.pallas_call(
        functools.partial(_merge_kernel, n_total=n, alpha=alpha),
        out_shape=jax.ShapeDtypeStruct((B, n, D), F32),
        grid=(B, n // tm),
        in_specs=[pl.BlockSpec((1, tm, D), lambda b, i: (b, i, 0)),
                  pl.BlockSpec((1, tm, BRANCH), lambda b, i: (b, i, 0)),
                  pl.BlockSpec((1, tm, BRANCH), lambda b, i: (b, i, 0)),
                  pl.BlockSpec((1, tm, MIX_W), lambda b, i: (b, i, 0)),
                  pl.BlockSpec((1, POOL_HALO, BRANCH),
                               lambda b, i: (b, jnp.maximum(i * hb - 1, 0), 0)),
                  pl.BlockSpec((1, POOL_HALO, BRANCH),
                               lambda b, i: (b, jnp.minimum((i + 1) * hb, nhb - 1), 0)),
                  pl.BlockSpec((1, 1, D), mod_map),
                  pl.BlockSpec((BRANCH, BRANCH), full2), row(BRANCH),
                  pl.BlockSpec((BRANCH, BRANCH), full2), row(BRANCH),
                  row(BRANCH), row(BRANCH),
                  pl.BlockSpec((SGU_HEADS, CHUNK, CHUNK), lambda b, i: (0, 0, 0)),
                  pl.BlockSpec((CHUNK, BRANCH), full2),
                  pl.BlockSpec((4 * BRANCH, D), full2), row(D), row(D)],
        out_specs=pl.BlockSpec((1, tm, D), lambda b, i: (b, i, 0)),
        scratch_shapes=[pltpu.VMEM((tm + 2 * POOL_HALO, BRANCH), F32) for _ in range(4)],
        compiler_params=pltpu.CompilerParams(
            dimension_semantics=("parallel", "parallel"), vmem_limit_bytes=V7X_VMEM_LIMIT),
        name="merge",
    )(x, att, y5, mix, mix, mix, gate, wpool, pscale, wglu, bglu, sg, sb, ws, bs, wout, lng, lnb)


def _prep_inproj_weights(w_in, g_q, w_uq, g_kv, w_ukv):
    D = w_in.shape[0]
    o_kr = KV_LORA
    o_s5 = o_kr + ROPE
    o_cq = o_s5 + BRANCH
    o_rest = o_cq + Q_LORA
    w = jnp.zeros((D, IN_PAD), F32)
    w = w.at[:, C_CKV:C_CKV + KV_LORA].set(w_in[:, 0:KV_LORA])
    w = w.at[:, C_CQ:C_CQ + Q_LORA].set(w_in[:, o_cq:o_cq + Q_LORA])
    w = w.at[:, C_KR:C_KR + ROPE].set(w_in[:, o_kr:o_kr + ROPE])
    w = w.at[:, C_S5:C_S5 + BRANCH].set(w_in[:, o_s5:o_s5 + BRANCH])
    w = w.at[:, C_MIX:C_MIX + MIX_W].set(w_in[:, o_rest:])
    hw = N_HEADS * HEAD_PAD
    scale = (NOPE + ROPE) ** -0.5
    ukv = w_ukv.reshape(KV_LORA, N_HEADS, NOPE + V_DIM)
    uq = w_uq.reshape(Q_LORA, N_HEADS, NOPE + ROPE) * scale
    wk = jnp.zeros((KV_LORA, N_HEADS, HEAD_PAD), F32).at[:, :, :NOPE].set(ukv[:, :, :NOPE])
    wv = jnp.zeros((KV_LORA, N_HEADS, HEAD_PAD), F32).at[:, :, :V_DIM].set(ukv[:, :, NOPE:])
    wq = jnp.zeros((Q_LORA, N_HEADS, HEAD_PAD), F32).at[:, :, :NOPE + ROPE].set(uq)
    pk = jnp.zeros((LANE, LANE), F32).at[jnp.arange(ROPE), NOPE + jnp.arange(ROPE)].set(1.0)
    vone = jnp.zeros((N_HEADS, HEAD_PAD), F32).at[:, V_DIM].set(1.0)
    return (w.astype(BF16), g_kv.reshape(1, KV_LORA), g_q.reshape(1, Q_LORA),
            wk.reshape(KV_LORA, hw).astype(BF16), wv.reshape(KV_LORA, hw).astype(BF16),
            wq.reshape(Q_LORA, hw).astype(BF16), pk.astype(BF16), vone.reshape(1, hw))


def _rope_tables(n, rotate):
    if not rotate:
        c = jnp.zeros((n, HEAD_PAD), F32).at[:, :NOPE + ROPE].set(1.0)
        z = jnp.zeros((n, HEAD_PAD), F32)
        return c, z, z
    nf = ROPE // 4
    t = jnp.arange(n)
    row = (t // GRID_W).astype(F32)
    col = (t % GRID_W).astype(F32)
    inv = ROPE_BASE ** (-jnp.arange(nf, dtype=F32) / nf)
    ang = jnp.stack([row[:, None] * inv, col[:, None] * inv], axis=1)
    cos, sin = jnp.cos(ang), jnp.sin(ang)
    zero = jnp.zeros_like(sin)
    cblk = jnp.concatenate([cos, cos], axis=-1).reshape(n, ROPE)
    s1blk = jnp.concatenate([-sin, zero], axis=-1).reshape(n, ROPE)
    s2blk = jnp.concatenate([zero, sin], axis=-1).reshape(n, ROPE)
    c = jnp.zeros((n, HEAD_PAD), F32).at[:, :NOPE].set(1.0).at[:, NOPE:NOPE + ROPE].set(cblk)
    s1 = jnp.zeros((n, HEAD_PAD), F32).at[:, NOPE:NOPE + ROPE].set(s1blk)
    s2 = jnp.zeros((n, HEAD_PAD), F32).at[:, NOPE:NOPE + ROPE].set(s2blk)
    return c, s1, s2


def _prep_s5(lam_re, lam_im, log_dt, b_re, b_im, c_re, c_im, s5_d):
    T, G, P, H = S5_T, S5_G, S5_P, S5_H
    lam_re = lam_re.astype(F32)
    lam_im = lam_im.astype(F32)
    dt = jnp.exp(log_dt.astype(F32))[..., None]
    zr, zi = lam_re * dt, lam_im * dt
    j = jnp.arange(T + 1, dtype=F32)[:, None, None, None]
    mag = jnp.exp(zr * j)
    pr, pi = mag * jnp.cos(zi * j), mag * jnp.sin(zi * j)
    nr, ni = pr[1] - 1.0, pi[1]
    den = lam_re * lam_re + lam_im * lam_im
    k_re = (nr * lam_re + ni * lam_im) / den
    k_im = (ni * lam_re - nr * lam_im) / den
    bb_re = k_re[..., None] * b_re - k_im[..., None] * b_im
    bb_im = k_re[..., None] * b_im + k_im[..., None] * b_re
    cl_re = c_re[None] * pr[:, :, :, None, :] - c_im[None] * pi[:, :, :, None, :]
    cl_im = c_re[None] * pi[:, :, :, None, :] + c_im[None] * pr[:, :, :, None, :]
    kk = (jnp.einsum('jdgop,dgph->jdgoh', cl_re[:T], bb_re)
          - jnp.einsum('jdgop,dgph->jdgoh', cl_im[:T], bb_im))
    tt = jnp.arange(T)
    dlt = tt[:, None] - tt[None, :]
    mf = jnp.where((dlt >= 0)[:, :, None, None, None], kk[jnp.clip(dlt, 0, T - 1), 0], 0.0)
    mb = jnp.where((dlt <= 0)[:, :, None, None, None], kk[jnp.clip(-dlt, 0, T - 1), 1], 0.0)
    m = (mf + mb).transpose(2, 0, 3, 1, 4).reshape(G, T * H, T * H)
    m = m + jnp.eye(T * H, dtype=F32)[None] * jnp.tile(s5_d.reshape(G, 1, H), (1, T, 1)).reshape(G, T * H, 1)
    pf_r, pf_i = pr[T - 1 - tt, 0], pi[T - 1 - tt, 0]
    pb_r, pb_i = pr[tt, 1], pi[tt, 1]
    def lb(p_r, p_i, d):
        re = p_r[..., None] * bb_re[d][None] - p_i[..., None] * bb_im[d][None]
        im = p_r[..., None] * bb_im[d][None] + p_i[..., None] * bb_re[d][None]
        to = lambda a: a.transpose(1, 2, 0, 3).reshape(G, P, T * H)
        return to(re), to(im)
    f_re, f_im = lb(pf_r, pf_i, 0)
    b_re_, b_im_ = lb(pb_r, pb_i, 1)
    pb = jnp.concatenate([f_re, b_re_, f_im, b_im_], axis=1)
    to_c = lambda a: a.transpose(1, 0, 2, 3).reshape(G, T * H, P)
    zf_re, zf_im = cl_re[1 + tt, 0], cl_im[1 + tt, 0]
    zb_re, zb_im = cl_re[T - tt, 1], cl_im[T - tt, 1]
    pc = jnp.concatenate([to_c(zf_re), to_c(zb_re), -to_c(zf_im), -to_c(zb_im)], axis=2)
    lanes = lambda a: jnp.concatenate([a[0], a[1]], axis=-1).reshape(1, G * 2 * P)
    return (m.astype(BF16), pb.astype(BF16), pc.astype(BF16), lanes(pr[T]), lanes(pi[T]))


def _prep_merge_weights(w_pool, pool_scale, w_glu, b_glu, sgu_g, sgu_b, w_s, b_s, w_out, ln_g, ln_b):
    wpool = jnp.zeros((BRANCH, BRANCH), F32)
    for gi in range(len(POOL_WINDOWS)):
        sl = slice(gi * POOL_GC, (gi + 1) * POOL_GC)
        wpool = wpool.at[sl, sl].set(w_pool[gi])
    bs = jnp.repeat(b_s.T, SGU_HD, axis=1)
    r = lambda a: a.reshape(1, -1).astype(F32)
    return (wpool.astype(BF16), r(pool_scale), w_glu.astype(BF16), r(b_glu), r(sgu_g), r(sgu_b),
            w_s.astype(BF16), bs.astype(F32), w_out.astype(BF16), r(ln_g), r(ln_b))


def kernel(x, c, ctx, c_ctx, w_mod, b_mod, w_in, g_q, w_uq, g_kv, w_ukv, w_pool, pool_scale, lam_re, lam_im, log_dt, s5_b_re, s5_b_im, s5_c_re, s5_c_im, s5_d, w_glu, b_glu, sgu_g, sgu_b, w_s, b_s, w_out, ln_g, ln_b):
    B, n, D = x.shape
    nc = ctx.shape[1]
    depth = w_mod.shape[0]
    alpha = (2 * depth) ** 0.25
    assert B + 1 <= 8
    tm = min(512, n)
    tmc = min(512, nc)
    tq = min(256, n)
    tqc = min(256, nc)

    cc = jnp.zeros((8, D), F32).at[:B].set(c.astype(F32)).at[B].set(c_ctx.astype(F32))
    mod = _modulation(cc, w_mod, b_mod.reshape(depth, 1, 3 * D))
    tab_l = _rope_tables(n, True)
    tab_c = _rope_tables(nc, False)

    for l in range(depth):
        last = l == depth - 1
        shift = mod[l, :, 0:D].reshape(8, 1, D)
        scale = mod[l, :, D:2 * D].reshape(8, 1, D)
        gate = mod[l, :, 2 * D:].reshape(8, 1, D)
        wts_in = _prep_inproj_weights(w_in[l], g_q[l], w_uq[l], g_kv[l], w_ukv[l])
        s5_ops = _prep_s5(lam_re[l], lam_im[l], log_dt[l], s5_b_re[l], s5_b_im[l],
                          s5_c_re[l], s5_c_im[l], s5_d[l])
        wts_mg = _prep_merge_weights(w_pool[l], pool_scale[l], w_glu[l], b_glu[l], sgu_g[l], sgu_b[l],
                                     w_s[l], b_s[l], w_out[l], ln_g[l], ln_b[l])

        q_l, k_l, v_l, s5u_l, mix_l = _inproj(x, shift[:B], scale[:B], True, wts_in, tab_l, tm)
        q_c, k_c, v_c, s5u_c, mix_c = _inproj(ctx, shift[B:B + 1], scale[B:B + 1], False, wts_in, tab_c, tmc)
        att_l = _attention(q_l, [k_l, k_c], [v_l, v_c], tq)
        y5_l, y5_c = _s5(s5u_l, s5u_c, s5_ops)
        x_new = _merge(x, att_l, y5_l, mix_l, gate[:B], True, wts_mg, tm, alpha)
        if not last:
            att_c = _attention(q_c, [k_c], [v_c], tqc)
            ctx = _merge(ctx, att_c, y5_c, mix_c, gate[B:B + 1], False, wts_mg, tmc, alpha)
        x = x_new
    return x
```

```python
import functools
import math

import jax
import jax.numpy as jnp
import numpy as np
from jax import lax
from jax.experimental import pallas as pl
from jax.experimental.pallas import tpu as pltpu

F32 = jnp.float32
BF16 = jnp.bfloat16

GRID_W = 64
N_HEADS = 4
NOPE = 64
ROPE = 32
V_DIM = 64
Q_LORA = 192
KV_LORA = 128
ROPE_BASE = 10000.0
BRANCH = 256
POOL_WINDOWS = (2, 4, 8, 16)
POOL_GC = 64
S5_H = 16
S5_G = 16
S5_P = 64
SGU_HEADS = 4
SGU_HD = 64
CHUNK = 128
LN_EPS = 1e-6

LANE = 128
HEAD_PAD = 128
V7X_VMEM_LIMIT = 56 * 1024 * 1024
S5_T = 16
S5_BLK = S5_T * LANE

C_CKV = 0
C_CQ = 128
C_KR = 384
C_S5 = 512
C_MIX = 768
IN_PAD = 2560
MIX_W = 3 * BRANCH + 4 * BRANCH


def _dot(a, b):
    return jnp.dot(a, b, preferred_element_type=F32)


def _dot_nt(a, b):
    return lax.dot_general(a, b, (((1,), (1,)), ((), ())), preferred_element_type=F32)


def _ln_rows(x):
    mu = jnp.mean(x, axis=-1, keepdims=True)
    xc = x - mu
    var = jnp.mean(xc * xc, axis=-1, keepdims=True)
    return xc * lax.rsqrt(var + LN_EPS)


def _sigmoid(x):
    return 1.0 / (1.0 + jnp.exp(-x))


def _mod_kernel(c_ref, w_ref, b_ref, o_ref):
    c = c_ref[...]
    s = c * _sigmoid(c)
    o_ref[0] = jnp.dot(s, w_ref[0], preferred_element_type=F32,
                       precision=lax.Precision.HIGHEST) + b_ref[0]


def _modulation(cc, w_mod, b_mod):
    L, D, D3 = w_mod.shape
    nb = D3 // D
    return pl.pallas_call(
        _mod_kernel,
        out_shape=jax.ShapeDtypeStruct((L, 8, D3), F32),
        grid=(L, nb),
        in_specs=[pl.BlockSpec((8, D), lambda l, j: (0, 0)),
                  pl.BlockSpec((1, D, D), lambda l, j: (l, 0, j)),
                  pl.BlockSpec((1, 1, D), lambda l, j: (l, 0, j))],
        out_specs=pl.BlockSpec((1, 8, D), lambda l, j: (l, 0, j)),
        compiler_params=pltpu.CompilerParams(vmem_limit_bytes=V7X_VMEM_LIMIT),
        name="modulation",
    )(cc, w_mod, b_mod)


def _rope_lanes(x, c, s1, s2):
    w = x.shape[-1]
    return x * c + pltpu.roll(x, w - 8, 1) * s1 + pltpu.roll(x, 8, 1) * s2


def _inproj_kernel(x_ref, shift_ref, scale_ref, w_ref, gkv_ref, gq_ref, wk_ref, wv_ref, wq_ref,
                   pk_ref, vone_ref, rc_ref, rs1_ref, rs2_ref,
                   q_ref, k_ref, vt_ref, s5_ref, mix_ref):
    x = x_ref[0]
    h = _ln_rows(x) * (1.0 + scale_ref[0]) + shift_ref[0]
    proj = _dot(h.astype(BF16), w_ref[...])
    ckv = proj[:, C_CKV:C_CKV + KV_LORA]
    cq = proj[:, C_CQ:C_CQ + Q_LORA]
    kr = proj[:, C_KR:C_KR + LANE]

    rk = ckv * lax.rsqrt(jnp.mean(ckv * ckv, axis=-1, keepdims=True) + LN_EPS) * gkv_ref[...]
    rk = rk.astype(BF16)
    rq = cq * lax.rsqrt(jnp.mean(cq * cq, axis=-1, keepdims=True) + LN_EPS) * gq_ref[...]
    rq = rq.astype(BF16)

    rc, rs1, rs2 = rc_ref[...], rs1_ref[...], rs2_ref[...]
    krp = _rope_lanes(_dot(kr.astype(BF16), pk_ref[...]), rc, rs1, rs2)
    kn = _dot(rk, wk_ref[...])
    vv = _dot(rk, wv_ref[...]) + vone_ref[...]
    qq = _dot(rq, wq_ref[...])
    for hd in range(N_HEADS):
        sl = slice(hd * HEAD_PAD, (hd + 1) * HEAD_PAD)
        q_ref[0, hd] = _rope_lanes(qq[:, sl], rc, rs1, rs2).astype(BF16)
        k_ref[0, hd] = (kn[:, sl] + krp).astype(BF16)
        vt_ref[0, hd, 0] = vv[:, sl].T.astype(BF16)
    s5_ref[0] = proj[:, C_S5:C_S5 + BRANCH].astype(BF16)
    mix_ref[0] = proj[:, C_MIX:C_MIX + MIX_W].astype(BF16)


def _inproj(x, shift, scale, batch_mod, wts, tables, tm):
    B, n, D = x.shape
    w_in, gkv, gq, wk, wv, wq, pk, vone = wts
    rc, rs1, rs2 = tables
    mod_map = (lambda b, i: (b, 0, 0)) if batch_mod else (lambda b, i: (0, 0, 0))
    full2 = lambda b, i: (0, 0)
    tab = pl.BlockSpec((tm, LANE), lambda b, i: (i, 0))
    hw = N_HEADS * HEAD_PAD
    return pl.pallas_call(
        _inproj_kernel,
        out_shape=(jax.ShapeDtypeStruct((B, N_HEADS, n, HEAD_PAD), BF16),
                   jax.ShapeDtypeStruct((B, N_HEADS, n, HEAD_PAD), BF16),
                   jax.ShapeDtypeStruct((B, N_HEADS, n // tm, HEAD_PAD, tm), BF16),
                   jax.ShapeDtypeStruct((B, n, BRANCH), BF16),
                   jax.ShapeDtypeStruct((B, n, MIX_W), BF16)),
        grid=(B, n // tm),
        in_specs=[pl.BlockSpec((1, tm, D), lambda b, i: (b, i, 0)),
                  pl.BlockSpec((1, 1, D), mod_map),
                  pl.BlockSpec((1, 1, D), mod_map),
                  pl.BlockSpec((D, IN_PAD), full2),
                  pl.BlockSpec((1, KV_LORA), full2),
                  pl.BlockSpec((1, Q_LORA), full2),
                  pl.BlockSpec((KV_LORA, hw), full2),
                  pl.BlockSpec((KV_LORA, hw), full2),
                  pl.BlockSpec((Q_LORA, hw), full2),
                  pl.BlockSpec((LANE, LANE), full2),
                  pl.BlockSpec((1, hw), full2),
                  tab, tab, tab],
        out_specs=(pl.BlockSpec((1, N_HEADS, tm, HEAD_PAD), lambda b, i: (b, 0, i, 0)),
                   pl.BlockSpec((1, N_HEADS, tm, HEAD_PAD), lambda b, i: (b, 0, i, 0)),
                   pl.BlockSpec((1, N_HEADS, 1, HEAD_PAD, tm), lambda b, i: (b, 0, i, 0, 0)),
                   pl.BlockSpec((1, tm, BRANCH), lambda b, i: (b, i, 0)),
                   pl.BlockSpec((1, tm, MIX_W), lambda b, i: (b, i, 0))),
        compiler_params=pltpu.CompilerParams(
            dimension_semantics=("parallel", "parallel"), vmem_limit_bytes=V7X_VMEM_LIMIT),
        name="inproj",
    )(x, shift, scale, w_in, gkv, gq, wk, wv, wq, pk, vone, rc, rs1, rs2)


ATT_TK = 512


def _attn_kernel(*refs, n_src):
    q_ref = refs[0]
    k_refs = refs[1:1 + n_src]
    vt_refs = refs[1 + n_src:1 + 2 * n_src]
    o_ref = refs[1 + 2 * n_src]
    s_scr, pa_scr, pb_scr, acc_scr, ot_scr = refs[2 + 2 * n_src:]
    tq = q_ref.shape[2]

    def head(hd, carry):
        q = q_ref[0, hd]
        m_acc = jnp.full((8, tq), -jnp.inf, F32)
        row = 0
        for si in range(n_src):
            nt, tk = vt_refs[si].shape[2], vt_refs[si].shape[4]
            for j in range(nt):
                s = _dot_nt(k_refs[si][0, hd, j * tk:(j + 1) * tk, :], q)
                s_scr[row:row + tk, :] = s
                parts = [s[r:r + 8, :] for r in range(0, tk, 8)]
                while len(parts) > 1:
                    parts = [jnp.maximum(parts[i], parts[i + 1]) for i in range(0, len(parts), 2)]
                m_acc = jnp.maximum(m_acc, parts[0])
                row += tk
        m = jnp.max(m_acc, axis=0, keepdims=True)

        acc_scr[...] = jnp.zeros(acc_scr.shape, F32)
        row = 0
        for si in range(n_src):
            vt_ref = vt_refs[si]
            nt, tk = vt_ref.shape[2], vt_ref.shape[4]
            row0 = row
            row += nt * tk

            def expo(j, dst, slot, row0=row0, tk=tk):
                r = pl.multiple_of(row0 + j * tk, 128)
                dst[slot, 0:tk, :] = jnp.exp2(s_scr[pl.ds(r, tk), :] - m).astype(BF16)

            def mm(j, src, slot, vt_ref=vt_ref, tk=tk):
                return _dot(vt_ref[0, hd, j], src[slot, 0:tk, :])

            if nt == 1:
                expo(0, pa_scr, 0)
                acc_scr[...] += mm(0, pa_scr, 0)
                continue
            npair = nt // 2
            expo(0, pa_scr, 0)
            expo(1, pa_scr, 1)

            def two_pairs(t, c):
                i = 2 * t
                acc_scr[...] += mm(2 * i, pa_scr, 0) + mm(2 * i + 1, pa_scr, 1)
                expo(2 * i + 2, pb_scr, 0)
                expo(2 * i + 3, pb_scr, 1)
                acc_scr[...] += mm(2 * i + 2, pb_scr, 0) + mm(2 * i + 3, pb_scr, 1)
                expo(2 * i + 4, pa_scr, 0)
                expo(2 * i + 5, pa_scr, 1)
                return c

            lax.fori_loop(0, (npair - 2) // 2, two_pairs, 0)
            acc_scr[...] += mm(nt - 4, pa_scr, 0) + mm(nt - 3, pa_scr, 1)
            expo(nt - 2, pb_scr, 0)
            expo(nt - 1, pb_scr, 1)
            acc_scr[...] += mm(nt - 2, pb_scr, 0) + mm(nt - 1, pb_scr, 1)
        acc = acc_scr[...]
        o = acc[:V_DIM, :] / acc[V_DIM:V_DIM + 1, :]
        ot_scr[pl.ds(pl.multiple_of(hd * V_DIM, V_DIM), V_DIM), :] = o
        return carry

    lax.fori_loop(0, N_HEADS, head, 0)
    o_ref[0] = ot_scr[...].T.astype(o_ref.dtype)


def _attention(q, ks, vts, tq):
    B, H, n, _ = q.shape
    n_src = len(ks)
    nkeys = sum(k.shape[2] for k in ks)
    tk_max = max(v.shape[4] for v in vts)
    for k, v in zip(ks, vts):
        nt = v.shape[2]
        assert k.shape[2] == nt * v.shape[4] and (nt == 1 or nt % 4 == 0)
    one_buf = pl.Buffered(1)
    k_spec = lambda a: pl.BlockSpec((1, H) + a.shape[2:], lambda b, i: (b, 0, 0, 0),
                                    pipeline_mode=one_buf)
    vt_spec = lambda a: pl.BlockSpec((1, H) + a.shape[2:], lambda b, i: (b, 0, 0, 0, 0),
                                     pipeline_mode=one_buf)
    return pl.pallas_call(
        functools.partial(_attn_kernel, n_src=n_src),
        out_shape=jax.ShapeDtypeStruct((B, n, H * V_DIM), BF16),
        grid=(B, n // tq),
        in_specs=[pl.BlockSpec((1, H, tq, HEAD_PAD), lambda b, i: (b, 0, i, 0))]
                 + [k_spec(k) for k in ks] + [vt_spec(v) for v in vts],
        out_specs=pl.BlockSpec((1, tq, H * V_DIM), lambda b, i: (b, i, 0)),
        scratch_shapes=[pltpu.VMEM((nkeys, tq), F32),
                        pltpu.VMEM((2, tk_max, tq), BF16), pltpu.VMEM((2, tk_max, tq), BF16),
                        pltpu.VMEM((HEAD_PAD, tq), F32), pltpu.VMEM((H * V_DIM, tq), F32)],
        compiler_params=pltpu.CompilerParams(
            dimension_semantics=("parallel", "parallel"), vmem_limit_bytes=V7X_VMEM_LIMIT),
        name="attention",
    )(q, *ks, *vts)


def _s5_kernel(ul_ref, uc_ref, m_ref, pb_ref, pc_ref, are_ref, aim_ref, yl_ref, yc_ref,
               ut_ref, sre_ref, sim_ref, f_ref):
    n = ul_ref.shape[1]
    nc = uc_ref.shape[1]
    kl = n // S5_T
    kc = nc // S5_T
    nblk = n // S5_BLK
    rows_blk = S5_BLK // S5_T
    ncol = ut_ref.shape[2]

    half_g = S5_G // 2

    def relayout_in(nrows, col0):
        for s in range(S5_T):
            for h in range(2):
                blk = f_ref[h, pl.ds(s, nrows, stride=S5_T), :]
                if nrows < LANE:
                    blk = jnp.concatenate([blk, jnp.zeros((LANE - nrows, LANE), F32)], axis=0)
                bt = blk.T
                for gg in range(half_g):
                    ut_ref[h * half_g + gg, s * S5_H:(s + 1) * S5_H, col0:col0 + LANE] = (
                        bt[gg * S5_H:(gg + 1) * S5_H, :].astype(BF16))

    for jb in range(nblk):
        for h in range(2):
            f_ref[h] = ul_ref[0, jb * S5_BLK:(jb + 1) * S5_BLK, h * LANE:(h + 1) * LANE].astype(F32)
        relayout_in(rows_blk, jb * LANE)
    for h in range(2):
        f_ref[h, 0:nc, :] = uc_ref[0, :, h * LANE:(h + 1) * LANE].astype(F32)
    relayout_in(kc, kl)

    for g in range(S5_G):
        st = _dot(pb_ref[g], ut_ref[g])
        stt = st.T
        sre_ref[:, g * LANE:(g + 1) * LANE] = stt[:, :LANE]
        sim_ref[:, g * LANE:(g + 1) * LANE] = stt[:, LANE:]

    are = are_ref[...]
    aim = aim_ref[...]
    lane = lax.broadcasted_iota(jnp.int32, (1, S5_G * LANE), 1)
    is_f = (lane % LANE) < S5_P

    def step(rf, rb, carry):
        hr, hi = carry
        srf, sif = sre_ref[pl.ds(rf, 1), :], sim_ref[pl.ds(rf, 1), :]
        srb, sib = sre_ref[pl.ds(rb, 1), :], sim_ref[pl.ds(rb, 1), :]
        sre_ref[pl.ds(rf, 1), :] = jnp.where(is_f, hr, srf)
        sim_ref[pl.ds(rf, 1), :] = jnp.where(is_f, hi, sif)
        sre_ref[pl.ds(rb, 1), :] = jnp.where(is_f, srb, hr)
        sim_ref[pl.ds(rb, 1), :] = jnp.where(is_f, sib, hi)
        sr = jnp.where(is_f, srf, srb)
        si = jnp.where(is_f, sif, sib)
        return (are * hr - aim * hi + sr, are * hi + aim * hr + si)

    zero = jnp.zeros((1, S5_G * LANE), F32)
    carry = lax.fori_loop(0, kc, lambda i, c: step(kl + i, kl + kc - 1 - i, c), (zero, zero))
    lax.fori_loop(0, kl, lambda i, c: step(i, kl - 1 - i, c), carry)

    for g in range(S5_G):
        hin = jnp.concatenate([sre_ref[:, g * LANE:(g + 1) * LANE],
                               sim_ref[:, g * LANE:(g + 1) * LANE]], axis=1)
        hin_t = hin.T.astype(BF16)
        y = _dot(m_ref[g], ut_ref[g]) + _dot(pc_ref[g], hin_t)
        ut_ref[g] = y.astype(BF16)

    def relayout_out(nrows, col0):
        for t in range(S5_T):
            for h in range(2):
                yt = jnp.concatenate(
                    [ut_ref[h * half_g + gg, t * S5_H:(t + 1) * S5_H, col0:col0 + LANE]
                     for gg in range(half_g)], axis=0).astype(F32)
                f_ref[h, pl.ds(t, nrows, stride=S5_T), :] = yt.T[:nrows]

    for jb in range(nblk):
        relayout_out(rows_blk, jb * LANE)
        for h in range(2):
            yl_ref[0, jb * S5_BLK:(jb + 1) * S5_BLK, h * LANE:(h + 1) * LANE] = (
                f_ref[h].astype(yl_ref.dtype))
    relayout_out(kc, kl)
    for h in range(2):
        yc_ref[0, :, h * LANE:(h + 1) * LANE] = f_ref[h, 0:nc, :].astype(yc_ref.dtype)


def _s5(ul, uc, ops):
    B, n, _ = ul.shape
    nc = uc.shape[1]
    m, pb, pc, are, aim = ops
    assert n % S5_BLK == 0 and nc % S5_T == 0 and nc // S5_T <= LANE and nc <= S5_BLK
    assert (n // S5_T) % 2 == 0 and (nc // S5_T) % 2 == 0
    ncol = n // S5_T + LANE
    w3 = lambda b: (0, 0, 0)
    return pl.pallas_call(
        _s5_kernel,
        out_shape=(jax.ShapeDtypeStruct((B, n, BRANCH), BF16),
                   jax.ShapeDtypeStruct((B, nc, BRANCH), BF16)),
        grid=(B,),
        in_specs=[pl.BlockSpec((1, n, BRANCH), lambda b: (b, 0, 0)),
                  pl.BlockSpec((1, nc, BRANCH), lambda b: (b, 0, 0)),
                  pl.BlockSpec(m.shape, w3), pl.BlockSpec(pb.shape, w3), pl.BlockSpec(pc.shape, w3),
                  pl.BlockSpec(are.shape, lambda b: (0, 0)), pl.BlockSpec(aim.shape, lambda b: (0, 0))],
        out_specs=(pl.BlockSpec((1, n, BRANCH), lambda b: (b, 0, 0)),
                   pl.BlockSpec((1, nc, BRANCH), lambda b: (b, 0, 0))),
        scratch_shapes=[pltpu.VMEM((S5_G, S5_T * S5_H, ncol), BF16),
                        pltpu.VMEM((ncol, S5_G * LANE), F32),
                        pltpu.VMEM((ncol, S5_G * LANE), F32),
                        pltpu.VMEM((2, S5_BLK, LANE), F32)],
        compiler_params=pltpu.CompilerParams(
            dimension_semantics=("parallel",), vmem_limit_bytes=V7X_VMEM_LIMIT),
        name="s5",
    )(ul, uc, m, pb, pc, are, aim)


POOL_HALO = 16


def _gelu_tanh(x):
    return 0.5 * x * (1.0 + jnp.tanh(math.sqrt(2.0 / math.pi) * (x + 0.044715 * (x * x * x))))


def _merge_kernel(x_ref, att_ref, y5_ref, mix_ref, prev_ref, next_ref, gate_ref,
                  wpool_ref, pscale_ref, wglu_ref, bglu_ref, sg_ref, sb_ref, ws_ref, bs_ref,
                  wout_ref, lng_ref, lnb_ref, o_ref, a_ref, b2_ref, b4_ref, b8_ref,
                  *, n_total, alpha):
    tm = x_ref.shape[1]
    i = pl.program_id(1)
    ni = pl.num_programs(1)
    hl = POOL_HALO

    pool_in = mix_ref[0, :, 0:BRANCH].astype(F32)
    a_ref[0:hl, :] = jnp.where(i > 0, prev_ref[0].astype(F32), 0.0)
    a_ref[hl:hl + tm, :] = pool_in
    a_ref[hl + tm:hl + tm + hl, :] = jnp.where(i < ni - 1, next_ref[0].astype(F32), 0.0)
    r2 = tm + 16
    b2_ref[8:8 + r2, :] = a_ref[7:7 + r2, :] + a_ref[8:8 + r2, :]
    r4 = tm + 12
    b4_ref[10:10 + r4, :] = b2_ref[9:9 + r4, :] + b2_ref[11:11 + r4, :]
    r8 = tm + 8
    b8_ref[12:12 + r8, :] = b4_ref[10:10 + r8, :] + b4_ref[14:14 + r8, :]
    c2 = b2_ref[hl:hl + tm, :]
    c4 = b4_ref[hl:hl + tm, :]
    c8 = b8_ref[hl:hl + tm, :]
    c16 = b8_ref[hl - 4:hl - 4 + tm, :] + b8_ref[hl + 4:hl + 4 + tm, :]
    lane = lax.broadcasted_iota(jnp.int32, (tm, BRANCH), 1)
    grp = lane // POOL_GC
    tot = jnp.where(grp == 0, c2, jnp.where(grp == 1, c4, jnp.where(grp == 2, c8, c16)))
    t = lax.broadcasted_iota(jnp.int32, (tm, BRANCH), 0) + i * tm
    half = jnp.left_shift(1, grp)
    cnt = jnp.minimum(t + half, n_total) - jnp.maximum(t - half, 0)
    pooled = tot / cnt.astype(F32) - pool_in
    pool_l = _dot(pooled.astype(BF16), wpool_ref[...]) * pscale_ref[...]

    g1 = _gelu_tanh(y5_ref[0].astype(F32))
    s5_l = g1 * _sigmoid(_dot(g1.astype(BF16), wglu_ref[...]) + bglu_ref[...])

    su = mix_ref[0, :, BRANCH:2 * BRANCH].astype(F32)
    sv = mix_ref[0, :, 2 * BRANCH:3 * BRANCH].astype(F32)
    vn = (_ln_rows(sv) * sg_ref[...] + sb_ref[...]).astype(BF16)
    lane_c = lax.broadcasted_iota(jnp.int32, (CHUNK, BRANCH), 1) // SGU_HD
    mixed = []
    for c in range(tm // CHUNK):
        vc = vn[c * CHUNK:(c + 1) * CHUNK, :]
        r = _dot(ws_ref[SGU_HEADS - 1], vc)
        for hd in range(SGU_HEADS - 2, -1, -1):
            r = jnp.where(lane_c == hd, _dot(ws_ref[hd], vc), r)
        mixed.append(r + bs_ref[...])
    sgu_l = su * jnp.concatenate(mixed, axis=0)

    gts = mix_ref[0, :, 3 * BRANCH:].astype(F32)
    cat = jnp.concatenate([att_ref[0].astype(F32), pool_l, s5_l, sgu_l], axis=-1)
    y = _dot((cat * (gts * _sigmoid(gts))).astype(BF16), wout_ref[...])
    z = alpha * x_ref[0] + gate_ref[0] * y
    o_ref[0] = _ln_rows(z) * lng_ref[...] + lnb_ref[...]


def _merge(x, att, y5, mix, gate, batch_mod, wts, tm, alpha):
    B, n, D = x.shape
    wpool, pscale, wglu, bglu, sg, sb, ws, bs, wout, lng, lnb = wts
    hb = tm // POOL_HALO
    nhb = n // POOL_HALO
    full2 = lambda b, i: (0, 0)
    mod_map = (lambda b, i: (b, 0, 0)) if batch_mod else (lambda b, i: (0, 0, 0))
    row = lambda w: pl.BlockSpec((1, w), full2)
    return pl.pallas_call(
        functools.partial(_merge_kernel, n_total=n, alpha=alpha),
        out_shape=jax.ShapeDtypeStruct((B, n, D), F32),
        grid=(B, n // tm),
        in_specs=[pl.BlockSpec((1, tm, D), lambda b, i: (b, i, 0)),
                  pl.BlockSpec((1, tm, BRANCH), lambda b, i: (b, i, 0)),
                  pl.BlockSpec((1, tm, BRANCH), lambda b, i: (b, i, 0)),
                  pl.BlockSpec((1, tm, MIX_W), lambda b, i: (b, i, 0)),
                  pl.BlockSpec((1, POOL_HALO, BRANCH),
                               lambda b, i: (b, jnp.maximum(i * hb - 1, 0), 0)),
                  pl.BlockSpec((1, POOL_HALO, BRANCH),
                               lambda b, i: (b, jnp.minimum((i + 1) * hb, nhb - 1), 0)),
                  pl.BlockSpec((1, 1, D), mod_map),
                  pl.BlockSpec((BRANCH, BRANCH), full2), row(BRANCH),
                  pl.BlockSpec((BRANCH, BRANCH), full2), row(BRANCH),
                  row(BRANCH), row(BRANCH),
                  pl.BlockSpec((SGU_HEADS, CHUNK, CHUNK), lambda b, i: (0, 0, 0)),
                  pl.BlockSpec((CHUNK, BRANCH), full2),
                  pl.BlockSpec((4 * BRANCH, D), full2), row(D), row(D)],
        out_specs=pl.BlockSpec((1, tm, D), lambda b, i: (b, i, 0)),
        scratch_shapes=[pltpu.VMEM((tm + 2 * POOL_HALO, BRANCH), F32) for _ in range(4)],
        compiler_params=pltpu.CompilerParams(
            dimension_semantics=("parallel", "parallel"), vmem_limit_bytes=V7X_VMEM_LIMIT),
        name="merge",
    )(x, att, y5, mix, mix, mix, gate, wpool, pscale, wglu, bglu, sg, sb, ws, bs, wout, lng, lnb)


def _prep_inproj_weights(w_in, g_q, w_uq, g_kv, w_ukv):
    D = w_in.shape[0]
    o_kr = KV_LORA
    o_s5 = o_kr + ROPE
    o_cq = o_s5 + BRANCH
    o_rest = o_cq + Q_LORA
    wb = w_in.astype(BF16)
    zpad = lambda k: jnp.zeros((D, k), BF16)
    w = jnp.concatenate(
        [wb[:, 0:KV_LORA], wb[:, o_cq:o_cq + Q_LORA], zpad(C_KR - C_CQ - Q_LORA),
         wb[:, o_kr:o_kr + ROPE], zpad(C_S5 - C_KR - ROPE), wb[:, o_s5:o_s5 + BRANCH],
         wb[:, o_rest:]], axis=1)
    assert w.shape == (D, IN_PAD)
    hw = N_HEADS * HEAD_PAD
    scale = (NOPE + ROPE) ** -0.5 * math.log2(math.e)
    ukv = w_ukv.reshape(KV_LORA, N_HEADS, NOPE + V_DIM)
    uq = w_uq.reshape(Q_LORA, N_HEADS, NOPE + ROPE) * scale
    padl = lambda a: jnp.pad(a, ((0, 0), (0, 0), (0, HEAD_PAD - a.shape[2])))
    wk = padl(ukv[:, :, :NOPE])
    wv = padl(ukv[:, :, NOPE:])
    wq = padl(uq)
    pk = (jnp.arange(LANE)[:, None] + NOPE == jnp.arange(LANE)[None, :]) & (jnp.arange(LANE)[:, None] < ROPE)
    vone = jnp.tile(jnp.arange(HEAD_PAD) == V_DIM, N_HEADS).astype(F32)
    return (w, g_kv.reshape(1, KV_LORA), g_q.reshape(1, Q_LORA),
            wk.reshape(KV_LORA, hw).astype(BF16), wv.reshape(KV_LORA, hw).astype(BF16),
            wq.reshape(Q_LORA, hw).astype(BF16), pk.astype(BF16), vone.reshape(1, hw))


def _rope_tables(n, rotate):
    ones = jnp.ones((n, NOPE), F32)
    zpad = jnp.zeros((n, HEAD_PAD - NOPE - ROPE), F32)
    if not rotate:
        c = jnp.concatenate([ones, jnp.ones((n, ROPE), F32), zpad], axis=1)
        z = jnp.zeros((n, HEAD_PAD), F32)
        return c, z, z
    nf = ROPE // 4
    t = jnp.arange(n)
    row = (t // GRID_W).astype(F32)
    col = (t % GRID_W).astype(F32)
    inv = ROPE_BASE ** (-jnp.arange(nf, dtype=F32) / nf)
    ang_r, ang_c = row[:, None] * inv, col[:, None] * inv
    zero = jnp.zeros((n, nf), F32)
    cr, sr, ccl, scl = jnp.cos(ang_r), jnp.sin(ang_r), jnp.cos(ang_c), jnp.sin(ang_c)
    znope = jnp.zeros((n, NOPE), F32)
    c = jnp.concatenate([ones, cr, cr, ccl, ccl, zpad], axis=1)
    s1 = jnp.concatenate([znope, -sr, zero, -scl, zero, zpad], axis=1)
    s2 = jnp.concatenate([znope, zero, sr, zero, scl, zpad], axis=1)
    return c, s1, s2


def _s5_toeplitz_selector():
    T = S5_T
    sel = np.zeros((T, T, 2 * T + 1), np.float32)
    for t in range(T):
        for s in range(T):
            if t >= s:
                sel[t, s, t - s] = 1.0
            if s >= t:
                sel[t, s, T + s - t] = 1.0
        sel[t, t, 2 * T] = 1.0
    return sel


def _prep_s5(lam_re, lam_im, log_dt, b_re, b_im, c_re, c_im, s5_d):
    T, G, P, H = S5_T, S5_G, S5_P, S5_H
    hi = lax.Precision.HIGHEST
    lam_re = lam_re.astype(F32)
    lam_im = lam_im.astype(F32)
    dt = jnp.exp(log_dt.astype(F32))[..., None]
    zr, zi = lam_re * dt, lam_im * dt
    j = jnp.arange(T + 1, dtype=F32)[:, None, None, None]
    mag = jnp.exp(zr * j)
    pr, pi = mag * jnp.cos(zi * j), mag * jnp.sin(zi * j)
    nr, ni = pr[1] - 1.0, pi[1]
    den = lam_re * lam_re + lam_im * lam_im
    k_re = (nr * lam_re + ni * lam_im) / den
    k_im = (ni * lam_re - nr * lam_im) / den
    bb_re = k_re[..., None] * b_re - k_im[..., None] * b_im
    bb_im = k_re[..., None] * b_im + k_im[..., None] * b_re
    cl_re = c_re[None] * pr[:, :, :, None, :] - c_im[None] * pi[:, :, :, None, :]
    cl_im = c_re[None] * pi[:, :, :, None, :] + c_im[None] * pr[:, :, :, None, :]
    kk = (jnp.einsum('jdgop,dgph->jdgoh', cl_re[:T], bb_re, precision=hi)
          - jnp.einsum('jdgop,dgph->jdgoh', cl_im[:T], bb_im, precision=hi))
    skip = jnp.eye(H, dtype=F32)[None] * s5_d.astype(F32).reshape(G, H, 1)
    k2 = jnp.concatenate([kk[:, 0], kk[:, 1], skip[None]], axis=0)
    m = jnp.einsum('tsj,jgoh->gtosh', jnp.asarray(_s5_toeplitz_selector()), k2,
                   precision=hi).reshape(G, T * H, T * H)
    tt = jnp.arange(T)
    over_s = lambda a: jnp.repeat(a.transpose(1, 2, 0), H, axis=2)
    over_h = lambda a: jnp.tile(a, (1, 1, T))

    def lb(p_r, p_i, d):
        ar, ai, br, bi = over_s(p_r), over_s(p_i), over_h(bb_re[d]), over_h(bb_im[d])
        return ar * br - ai * bi, ar * bi + ai * br

    f_re, f_im = lb(pr[T - 1 - tt, 0], pi[T - 1 - tt, 0], 0)
    b_re_, b_im_ = lb(pr[tt, 1], pi[tt, 1], 1)
    pb = jnp.concatenate([f_re, b_re_, f_im, b_im_], axis=1)
    to_c = lambda a: a.transpose(1, 0, 2, 3).reshape(G, T * H, P)
    zf_re, zf_im = cl_re[1 + tt, 0], cl_im[1 + tt, 0]
    zb_re, zb_im = cl_re[T - tt, 1], cl_im[T - tt, 1]
    pc = jnp.concatenate([to_c(zf_re), to_c(zb_re), -to_c(zf_im), -to_c(zb_im)], axis=2)
    lanes = lambda a: jnp.concatenate([a[0], a[1]], axis=-1).reshape(1, G * 2 * P)
    return (m.astype(BF16), pb.astype(BF16), pc.astype(BF16), lanes(pr[T]), lanes(pi[T]))


def _prep_merge_weights(w_pool, pool_scale, w_glu, b_glu, sgu_g, sgu_b, w_s, b_s, w_out, ln_g, ln_b):
    z = jnp.zeros((POOL_GC, POOL_GC), F32)
    ng = len(POOL_WINDOWS)
    wpool = jnp.concatenate(
        [jnp.concatenate([w_pool[gi] if gj == gi else z for gj in range(ng)], axis=1)
         for gi in range(ng)], axis=0)
    bs = jnp.repeat(b_s.T, SGU_HD, axis=1)
    r = lambda a: a.reshape(1, -1).astype(F32)
    return (wpool.astype(BF16), r(pool_scale), w_glu.astype(BF16), r(b_glu), r(sgu_g), r(sgu_b),
            w_s.astype(BF16), bs.astype(F32), w_out.astype(BF16), r(ln_g), r(ln_b))


def kernel(x, c, ctx, c_ctx, w_mod, b_mod, w_in, g_q, w_uq, g_kv, w_ukv, w_pool, pool_scale, lam_re, lam_im, log_dt, s5_b_re, s5_b_im, s5_c_re, s5_c_im, s5_d, w_glu, b_glu, sgu_g, sgu_b, w_s, b_s, w_out, ln_g, ln_b):
    B, n, D = x.shape
    nc = ctx.shape[1]
    depth = w_mod.shape[0]
    alpha = (2 * depth) ** 0.25
    assert B + 1 <= 8
    tm = min(ATT_TK, n)
    tmc = min(ATT_TK, nc)
    tq = min(512, n)
    tqc = min(512, nc)

    cc = jnp.concatenate([c.astype(F32), c_ctx.astype(F32).reshape(1, D),
                          jnp.zeros((8 - B - 1, D), F32)], axis=0)
    mod = _modulation(cc, w_mod, b_mod.reshape(depth, 1, 3 * D))
    tab_l = _rope_tables(n, True)
    tab_c = _rope_tables(nc, False)

    for l in range(depth):
        last = l == depth - 1
        shift = mod[l, :, 0:D].reshape(8, 1, D)
        scale = mod[l, :, D:2 * D].reshape(8, 1, D)
        gate = mod[l, :, 2 * D:].reshape(8, 1, D)
        wts_in = _prep_inproj_weights(w_in[l], g_q[l], w_uq[l], g_kv[l], w_ukv[l])
        s5_ops = _prep_s5(lam_re[l], lam_im[l], log_dt[l], s5_b_re[l], s5_b_im[l],
                          s5_c_re[l], s5_c_im[l], s5_d[l])
        wts_mg = _prep_merge_weights(w_pool[l], pool_scale[l], w_glu[l], b_glu[l], sgu_g[l], sgu_b[l],
                                     w_s[l], b_s[l], w_out[l], ln_g[l], ln_b[l])

        q_l, k_l, v_l, s5u_l, mix_l = _inproj(x, shift[:B], scale[:B], True, wts_in, tab_l, tm)
        q_c, k_c, v_c, s5u_c, mix_c = _inproj(ctx, shift[B:B + 1], scale[B:B + 1], False, wts_in, tab_c, tmc)
        att_l = _attention(q_l, [k_l, k_c], [v_l, v_c], tq)
        y5_l, y5_c = _s5(s5u_l, s5u_c, s5_ops)
        x_new = _merge(x, att_l, y5_l, mix_l, gate[:B], True, wts_mg, tm, alpha)
        if not last:
            att_c = _attention(q_c, [k_c], [v_c], tqc)
            ctx = _merge(ctx, att_c, y5_c, mix_c, gate[B:B + 1], False, wts_mg, tmc, alpha)
        x = x_new
    return x
```

```python
import functools
import math

import jax
import jax.numpy as jnp
import numpy as np
from jax import lax
from jax.experimental import pallas as pl
from jax.experimental.pallas import tpu as pltpu

F32 = jnp.float32
BF16 = jnp.bfloat16

GRID_W = 64
N_HEADS = 4
NOPE = 64
ROPE = 32
V_DIM = 64
Q_LORA = 192
KV_LORA = 128
ROPE_BASE = 10000.0
BRANCH = 256
POOL_WINDOWS = (2, 4, 8, 16)
POOL_GC = 64
S5_H = 16
S5_G = 16
S5_P = 64
SGU_HEADS = 4
SGU_HD = 64
CHUNK = 128
LN_EPS = 1e-6

LANE = 128
HEAD_PAD = 128
V_ROWS = 80
V7X_VMEM_LIMIT = 56 * 1024 * 1024
S5_T = 16
S5_BLK = S5_T * LANE

C_CKV = 0
C_CQ = 128
C_KR = 384
C_S5 = 512
C_MIX = 768
IN_PAD = 2560
MIX_W = 3 * BRANCH + 4 * BRANCH


def _dot(a, b):
    return jnp.dot(a, b, preferred_element_type=F32)


def _dot_nt(a, b):
    return lax.dot_general(a, b, (((1,), (1,)), ((), ())), preferred_element_type=F32)


def _ln_rows(x):
    mu = jnp.mean(x, axis=-1, keepdims=True)
    xc = x - mu
    var = jnp.mean(xc * xc, axis=-1, keepdims=True)
    return xc * lax.rsqrt(var + LN_EPS)


def _sigmoid(x):
    return 1.0 / (1.0 + jnp.exp(-x))


def _mod_kernel(c_ref, w_ref, b_ref, o_ref):
    c = c_ref[...]
    s = c * _sigmoid(c)
    o_ref[0] = jnp.dot(s, w_ref[0], preferred_element_type=F32,
                       precision=lax.Precision.HIGHEST) + b_ref[0]


def _modulation(cc, w_mod, b_mod):
    L, D, D3 = w_mod.shape
    nb = D3 // D
    return pl.pallas_call(
        _mod_kernel,
        out_shape=jax.ShapeDtypeStruct((L, 8, D3), F32),
        grid=(L, nb),
        in_specs=[pl.BlockSpec((8, D), lambda l, j: (0, 0)),
                  pl.BlockSpec((1, D, D), lambda l, j: (l, 0, j)),
                  pl.BlockSpec((1, 1, D), lambda l, j: (l, 0, j))],
        out_specs=pl.BlockSpec((1, 8, D), lambda l, j: (l, 0, j)),
        compiler_params=pltpu.CompilerParams(vmem_limit_bytes=V7X_VMEM_LIMIT),
        name="modulation",
    )(cc, w_mod, b_mod)


def _rope_lanes(x, c, s1, s2):
    w = x.shape[-1]
    return x * c + pltpu.roll(x, w - 8, 1) * s1 + pltpu.roll(x, 8, 1) * s2


def _inproj_kernel(x_ref, shift_ref, scale_ref, w_ref, gkv_ref, gq_ref, wk_ref, wv_ref, wq_ref,
                   pk_ref, vone_ref, rc_ref, rs1_ref, rs2_ref,
                   q_ref, k_ref, vt_ref, s5_ref, mix_ref):
    x = x_ref[0]
    h = _ln_rows(x) * (1.0 + scale_ref[0]) + shift_ref[0]
    proj = _dot(h.astype(BF16), w_ref[...])
    ckv = proj[:, C_CKV:C_CKV + KV_LORA]
    cq = proj[:, C_CQ:C_CQ + Q_LORA]
    kr = proj[:, C_KR:C_KR + LANE]

    rk = ckv * lax.rsqrt(jnp.mean(ckv * ckv, axis=-1, keepdims=True) + LN_EPS) * gkv_ref[...]
    rk = rk.astype(BF16)
    rq = cq * lax.rsqrt(jnp.mean(cq * cq, axis=-1, keepdims=True) + LN_EPS) * gq_ref[...]
    rq = rq.astype(BF16)

    rc, rs1, rs2 = rc_ref[...], rs1_ref[...], rs2_ref[...]
    krp = _rope_lanes(_dot(kr.astype(BF16), pk_ref[...]), rc, rs1, rs2)
    kn = _dot(rk, wk_ref[...])
    vv = _dot(rk, wv_ref[...]) + vone_ref[...]
    qq = _dot(rq, wq_ref[...])
    for hd in range(N_HEADS):
        sl = slice(hd * HEAD_PAD, (hd + 1) * HEAD_PAD)
        q_ref[0, hd] = _rope_lanes(qq[:, sl], rc, rs1, rs2).astype(BF16)
        k_ref[0, hd] = (kn[:, sl] + krp).astype(BF16)
        vt_ref[0, hd, 0] = vv[:, sl].T[:V_ROWS].astype(BF16)
    s5_ref[0] = proj[:, C_S5:C_S5 + BRANCH].astype(BF16)
    mix_ref[0] = proj[:, C_MIX:C_MIX + MIX_W].astype(BF16)


def _inproj(x, shift, scale, batch_mod, wts, tables, tm):
    B, n, D = x.shape
    w_in, gkv, gq, wk, wv, wq, pk, vone = wts
    rc, rs1, rs2 = tables
    mod_map = (lambda b, i: (b, 0, 0)) if batch_mod else (lambda b, i: (0, 0, 0))
    full2 = lambda b, i: (0, 0)
    tab = pl.BlockSpec((tm, LANE), lambda b, i: (i, 0))
    hw = N_HEADS * HEAD_PAD
    return pl.pallas_call(
        _inproj_kernel,
        out_shape=(jax.ShapeDtypeStruct((B, N_HEADS, n, HEAD_PAD), BF16),
                   jax.ShapeDtypeStruct((B, N_HEADS, n, HEAD_PAD), BF16),
                   jax.ShapeDtypeStruct((B, N_HEADS, n // tm, V_ROWS, tm), BF16),
                   jax.ShapeDtypeStruct((B, n, BRANCH), BF16),
                   jax.ShapeDtypeStruct((B, n, MIX_W), BF16)),
        grid=(B, n // tm),
        in_specs=[pl.BlockSpec((1, tm, D), lambda b, i: (b, i, 0)),
                  pl.BlockSpec((1, 1, D), mod_map),
                  pl.BlockSpec((1, 1, D), mod_map),
                  pl.BlockSpec((D, IN_PAD), full2),
                  pl.BlockSpec((1, KV_LORA), full2),
                  pl.BlockSpec((1, Q_LORA), full2),
                  pl.BlockSpec((KV_LORA, hw), full2),
                  pl.BlockSpec((KV_LORA, hw), full2),
                  pl.BlockSpec((Q_LORA, hw), full2),
                  pl.BlockSpec((LANE, LANE), full2),
                  pl.BlockSpec((1, hw), full2),
                  tab, tab, tab],
        out_specs=(pl.BlockSpec((1, N_HEADS, tm, HEAD_PAD), lambda b, i: (b, 0, i, 0)),
                   pl.BlockSpec((1, N_HEADS, tm, HEAD_PAD), lambda b, i: (b, 0, i, 0)),
                   pl.BlockSpec((1, N_HEADS, 1, V_ROWS, tm), lambda b, i: (b, 0, i, 0, 0)),
                   pl.BlockSpec((1, tm, BRANCH), lambda b, i: (b, i, 0)),
                   pl.BlockSpec((1, tm, MIX_W), lambda b, i: (b, i, 0))),
        compiler_params=pltpu.CompilerParams(
            dimension_semantics=("parallel", "parallel"), vmem_limit_bytes=V7X_VMEM_LIMIT),
        name="inproj",
    )(x, shift, scale, w_in, gkv, gq, wk, wv, wq, pk, vone, rc, rs1, rs2)


ATT_TK = 512


def _attn_kernel(*refs, n_src):
    q_ref = refs[0]
    k_refs = refs[1:1 + n_src]
    vt_refs = refs[1 + n_src:1 + 2 * n_src]
    o_ref = refs[1 + 2 * n_src]
    s_scr, pa_scr, pb_scr, acc_scr, m_scr = refs[2 + 2 * n_src:]
    tq = q_ref.shape[2]
    step = pl.program_id(1)

    @pl.when(step == 0)
    def _():
        s_scr[...] = jnp.zeros(s_scr.shape, F32)
        m_scr[...] = jnp.zeros(m_scr.shape, F32)

    q = q_ref[0, 0]
    m_prev = jnp.max(m_scr[...], axis=0, keepdims=True)
    acc_scr[...] = jnp.zeros(acc_scr.shape, F32)
    m_acc = jnp.full((8, tq), -jnp.inf, F32)

    tiles = []
    row = 0
    for si in range(n_src):
        nt, tk = vt_refs[si].shape[2], vt_refs[si].shape[4]
        for j in range(nt):
            tiles.append((si, j, row, tk))
            row += tk
    p_bufs = ((pa_scr, 0), (pa_scr, 1), (pb_scr, 0), (pb_scr, 1))

    def ex(i):
        si, j, r, tk = tiles[i]
        buf, slot = p_bufs[i % len(p_bufs)]
        buf[slot, 0:tk, :] = jnp.exp2(s_scr[r:r + tk, :] - m_prev).astype(BF16)

    def qk(i, m):
        si, j, r, tk = tiles[i]
        s = _dot_nt(k_refs[si][0, 0, j * tk:(j + 1) * tk, :], q)
        s_scr[r:r + tk, :] = s
        parts = [s[a:a + 8, :] for a in range(0, tk, 8)]
        while len(parts) > 1:
            parts = [jnp.maximum(parts[a], parts[a + 1]) for a in range(0, len(parts), 2)]
        return jnp.maximum(m, parts[0])

    def mm(i):
        si, j, r, tk = tiles[i]
        buf, slot = p_bufs[i % len(p_bufs)]
        acc_scr[...] += _dot(vt_refs[si][0, 0, j], buf[slot, 0:tk, :])

    ex(0)
    for i in range(len(tiles)):
        if i + 1 < len(tiles):
            ex(i + 1)
        m_acc = qk(i, m_acc)
        mm(i)

    m_scr[...] = m_acc

    @pl.when(step > 0)
    def _():
        acc = acc_scr[...]
        o_ref[0, 0] = (acc[:V_DIM, :] / acc[V_DIM:V_DIM + 1, :]).astype(o_ref.dtype)


def _attention(q, ks, vts, tq):
    B, H, n, _ = q.shape
    n_src = len(ks)
    nkeys = sum(k.shape[2] for k in ks)
    tk_max = max(v.shape[4] for v in vts)
    for k, v in zip(ks, vts):
        nt = v.shape[2]
        assert k.shape[2] == nt * v.shape[4]
    nq = n // tq
    n_items = H * nq
    cur = lambda k: jnp.minimum(k, n_items - 1)
    prv = lambda k: jnp.maximum(k - 1, 0)
    k_spec = lambda a: pl.BlockSpec((1, 1) + a.shape[2:], lambda b, k: (b, cur(k) // nq, 0, 0))
    vt_spec = lambda a: pl.BlockSpec((1, 1) + a.shape[2:], lambda b, k: (b, prv(k) // nq, 0, 0, 0))
    return pl.pallas_call(
        functools.partial(_attn_kernel, n_src=n_src),
        out_shape=jax.ShapeDtypeStruct((B, H, V_DIM, n), BF16),
        grid=(B, n_items + 1),
        in_specs=[pl.BlockSpec((1, 1, tq, HEAD_PAD), lambda b, k: (b, cur(k) // nq, cur(k) % nq, 0))]
                 + [k_spec(k) for k in ks] + [vt_spec(v) for v in vts],
        out_specs=pl.BlockSpec((1, 1, V_DIM, tq), lambda b, k: (b, prv(k) // nq, 0, prv(k) % nq)),
        scratch_shapes=[pltpu.VMEM((nkeys, tq), F32),
                        pltpu.VMEM((2, tk_max, tq), BF16), pltpu.VMEM((2, tk_max, tq), BF16),
                        pltpu.VMEM((V_ROWS, tq), F32), pltpu.VMEM((8, tq), F32)],
        compiler_params=pltpu.CompilerParams(
            dimension_semantics=("parallel", "arbitrary"), vmem_limit_bytes=V7X_VMEM_LIMIT),
        name="attention",
    )(q, *ks, *vts)


def _s5_kernel(ul_ref, uc_ref, m_ref, pb_ref, pc_ref, are_ref, aim_ref, yl_ref, yc_ref,
               ut_ref, sre_ref, sim_ref, f_ref):
    n = ul_ref.shape[1]
    nc = uc_ref.shape[1]
    kl = n // S5_T
    kc = nc // S5_T
    nblk = n // S5_BLK
    rows_blk = S5_BLK // S5_T
    ncol = ut_ref.shape[2]

    half_g = S5_G // 2

    def relayout_in(nrows, col0):
        for s in range(S5_T):
            for h in range(2):
                blk = f_ref[h, pl.ds(s, nrows, stride=S5_T), :]
                if nrows < LANE:
                    blk = jnp.concatenate([blk, jnp.zeros((LANE - nrows, LANE), F32)], axis=0)
                bt = blk.T
                for gg in range(half_g):
                    ut_ref[h * half_g + gg, s * S5_H:(s + 1) * S5_H, col0:col0 + LANE] = (
                        bt[gg * S5_H:(gg + 1) * S5_H, :].astype(BF16))

    for jb in range(nblk):
        for h in range(2):
            f_ref[h] = ul_ref[0, jb * S5_BLK:(jb + 1) * S5_BLK, h * LANE:(h + 1) * LANE].astype(F32)
        relayout_in(rows_blk, jb * LANE)
    for h in range(2):
        f_ref[h, 0:nc, :] = uc_ref[0, :, h * LANE:(h + 1) * LANE].astype(F32)
    relayout_in(kc, kl)

    for g in range(S5_G):
        st = _dot(pb_ref[g], ut_ref[g])
        stt = st.T
        sre_ref[:, g * LANE:(g + 1) * LANE] = stt[:, :LANE]
        sim_ref[:, g * LANE:(g + 1) * LANE] = stt[:, LANE:]

    are = are_ref[...]
    aim = aim_ref[...]
    lane = lax.broadcasted_iota(jnp.int32, (1, S5_G * LANE), 1)
    is_f = (lane % LANE) < S5_P

    def step(rf, rb, carry):
        hr, hi = carry
        srf, sif = sre_ref[pl.ds(rf, 1), :], sim_ref[pl.ds(rf, 1), :]
        srb, sib = sre_ref[pl.ds(rb, 1), :], sim_ref[pl.ds(rb, 1), :]
        sre_ref[pl.ds(rf, 1), :] = jnp.where(is_f, hr, srf)
        sim_ref[pl.ds(rf, 1), :] = jnp.where(is_f, hi, sif)
        sre_ref[pl.ds(rb, 1), :] = jnp.where(is_f, srb, hr)
        sim_ref[pl.ds(rb, 1), :] = jnp.where(is_f, sib, hi)
        sr = jnp.where(is_f, srf, srb)
        si = jnp.where(is_f, sif, sib)
        return (are * hr - aim * hi + sr, are * hi + aim * hr + si)

    zero = jnp.zeros((1, S5_G * LANE), F32)
    carry = lax.fori_loop(0, kc, lambda i, c: step(kl + i, kl + kc - 1 - i, c), (zero, zero))
    lax.fori_loop(0, kl, lambda i, c: step(i, kl - 1 - i, c), carry)

    for g in range(S5_G):
        hin = jnp.concatenate([sre_ref[:, g * LANE:(g + 1) * LANE],
                               sim_ref[:, g * LANE:(g + 1) * LANE]], axis=1)
        hin_t = hin.T.astype(BF16)
        y = _dot(m_ref[g], ut_ref[g]) + _dot(pc_ref[g], hin_t)
        ut_ref[g] = y.astype(BF16)

    def relayout_out(nrows, col0):
        for t in range(S5_T):
            for h in range(2):
                yt = jnp.concatenate(
                    [ut_ref[h * half_g + gg, t * S5_H:(t + 1) * S5_H, col0:col0 + LANE]
                     for gg in range(half_g)], axis=0).astype(F32)
                f_ref[h, pl.ds(t, nrows, stride=S5_T), :] = yt.T[:nrows]

    for jb in range(nblk):
        relayout_out(rows_blk, jb * LANE)
        for h in range(2):
            yl_ref[0, jb * S5_BLK:(jb + 1) * S5_BLK, h * LANE:(h + 1) * LANE] = (
                f_ref[h].astype(yl_ref.dtype))
    relayout_out(kc, kl)
    for h in range(2):
        yc_ref[0, :, h * LANE:(h + 1) * LANE] = f_ref[h, 0:nc, :].astype(yc_ref.dtype)


def _s5(ul, uc, ops):
    B, n, _ = ul.shape
    nc = uc.shape[1]
    m, pb, pc, are, aim = ops
    assert n % S5_BLK == 0 and nc % S5_T == 0 and nc // S5_T <= LANE and nc <= S5_BLK
    assert (n // S5_T) % 2 == 0 and (nc // S5_T) % 2 == 0
    ncol = n // S5_T + LANE
    w3 = lambda b: (0, 0, 0)
    return pl.pallas_call(
        _s5_kernel,
        out_shape=(jax.ShapeDtypeStruct((B, n, BRANCH), BF16),
                   jax.ShapeDtypeStruct((B, nc, BRANCH), BF16)),
        grid=(B,),
        in_specs=[pl.BlockSpec((1, n, BRANCH), lambda b: (b, 0, 0)),
                  pl.BlockSpec((1, nc, BRANCH), lambda b: (b, 0, 0)),
                  pl.BlockSpec(m.shape, w3), pl.BlockSpec(pb.shape, w3), pl.BlockSpec(pc.shape, w3),
                  pl.BlockSpec(are.shape, lambda b: (0, 0)), pl.BlockSpec(aim.shape, lambda b: (0, 0))],
        out_specs=(pl.BlockSpec((1, n, BRANCH), lambda b: (b, 0, 0)),
                   pl.BlockSpec((1, nc, BRANCH), lambda b: (b, 0, 0))),
        scratch_shapes=[pltpu.VMEM((S5_G, S5_T * S5_H, ncol), BF16),
                        pltpu.VMEM((ncol, S5_G * LANE), F32),
                        pltpu.VMEM((ncol, S5_G * LANE), F32),
                        pltpu.VMEM((2, S5_BLK, LANE), F32)],
        compiler_params=pltpu.CompilerParams(
            dimension_semantics=("parallel",), vmem_limit_bytes=V7X_VMEM_LIMIT),
        name="s5",
    )(ul, uc, m, pb, pc, are, aim)


POOL_HALO = 16


def _gelu_tanh(x):
    return 0.5 * x * (1.0 + jnp.tanh(math.sqrt(2.0 / math.pi) * (x + 0.044715 * (x * x * x))))


def _merge_kernel(x_ref, att_ref, y5_ref, mix_ref, prev_ref, next_ref, gate_ref,
                  wpool_ref, pscale_ref, wglu_ref, bglu_ref, sg_ref, sb_ref, ws_ref, bs_ref,
                  wout_ref, lng_ref, lnb_ref, o_ref, a_ref, b2_ref, b4_ref, b8_ref,
                  *, n_total, alpha):
    tm = x_ref.shape[1]
    i = pl.program_id(1)
    ni = pl.num_programs(1)
    hl = POOL_HALO

    pool_in = mix_ref[0, :, 0:BRANCH].astype(F32)
    a_ref[0:hl, :] = jnp.where(i > 0, prev_ref[0].astype(F32), 0.0)
    a_ref[hl:hl + tm, :] = pool_in
    a_ref[hl + tm:hl + tm + hl, :] = jnp.where(i < ni - 1, next_ref[0].astype(F32), 0.0)
    r2 = tm + 16
    b2_ref[8:8 + r2, :] = a_ref[7:7 + r2, :] + a_ref[8:8 + r2, :]
    r4 = tm + 12
    b4_ref[10:10 + r4, :] = b2_ref[9:9 + r4, :] + b2_ref[11:11 + r4, :]
    r8 = tm + 8
    b8_ref[12:12 + r8, :] = b4_ref[10:10 + r8, :] + b4_ref[14:14 + r8, :]
    c2 = b2_ref[hl:hl + tm, :]
    c4 = b4_ref[hl:hl + tm, :]
    c8 = b8_ref[hl:hl + tm, :]
    c16 = b8_ref[hl - 4:hl - 4 + tm, :] + b8_ref[hl + 4:hl + 4 + tm, :]
    lane = lax.broadcasted_iota(jnp.int32, (tm, BRANCH), 1)
    grp = lane // POOL_GC
    tot = jnp.where(grp == 0, c2, jnp.where(grp == 1, c4, jnp.where(grp == 2, c8, c16)))
    t = lax.broadcasted_iota(jnp.int32, (tm, BRANCH), 0) + i * tm
    half = jnp.left_shift(1, grp)
    cnt = jnp.minimum(t + half, n_total) - jnp.maximum(t - half, 0)
    pooled = tot / cnt.astype(F32) - pool_in
    pool_l = _dot(pooled.astype(BF16), wpool_ref[...]) * pscale_ref[...]

    g1 = _gelu_tanh(y5_ref[0].astype(F32))
    s5_l = g1 * _sigmoid(_dot(g1.astype(BF16), wglu_ref[...]) + bglu_ref[...])

    su = mix_ref[0, :, BRANCH:2 * BRANCH].astype(F32)
    sv = mix_ref[0, :, 2 * BRANCH:3 * BRANCH].astype(F32)
    vn = (_ln_rows(sv) * sg_ref[...] + sb_ref[...]).astype(BF16)
    lane_c = lax.broadcasted_iota(jnp.int32, (CHUNK, BRANCH), 1) // SGU_HD
    mixed = []
    for c in range(tm // CHUNK):
        vc = vn[c * CHUNK:(c + 1) * CHUNK, :]
        r = _dot(ws_ref[SGU_HEADS - 1], vc)
        for hd in range(SGU_HEADS - 2, -1, -1):
            r = jnp.where(lane_c == hd, _dot(ws_ref[hd], vc), r)
        mixed.append(r + bs_ref[...])
    sgu_l = su * jnp.concatenate(mixed, axis=0)

    gts = mix_ref[0, :, 3 * BRANCH:].astype(F32)
    att = att_ref[0].astype(F32).reshape(N_HEADS * V_DIM, tm).T
    cat = jnp.concatenate([att, pool_l, s5_l, sgu_l], axis=-1)
    y = _dot((cat * (gts * _sigmoid(gts))).astype(BF16), wout_ref[...])
    z = alpha * x_ref[0] + gate_ref[0] * y
    o_ref[0] = _ln_rows(z) * lng_ref[...] + lnb_ref[...]


def _merge(x, att, y5, mix, gate, batch_mod, wts, tm, alpha):
    B, n, D = x.shape
    wpool, pscale, wglu, bglu, sg, sb, ws, bs, wout, lng, lnb = wts
    hb = tm // POOL_HALO
    nhb = n // POOL_HALO
    full2 = lambda b, i: (0, 0)
    mod_map = (lambda b, i: (b, 0, 0)) if batch_mod else (lambda b, i: (0, 0, 0))
    row = lambda w: pl.BlockSpec((1, w), full2)
    return pl.pallas_call(
        functools.partial(_merge_kernel, n_total=n, alpha=alpha),
        out_shape=jax.ShapeDtypeStruct((B, n, D), F32),
        grid=(B, n // tm),
        in_specs=[pl.BlockSpec((1, tm, D), lambda b, i: (b, i, 0)),
                  pl.BlockSpec((1, N_HEADS, V_DIM, tm), lambda b, i: (b, 0, 0, i)),
                  pl.BlockSpec((1, tm, BRANCH), lambda b, i: (b, i, 0)),
                  pl.BlockSpec((1, tm, MIX_W), lambda b, i: (b, i, 0)),
                  pl.BlockSpec((1, POOL_HALO, BRANCH),
                               lambda b, i: (b, jnp.maximum(i * hb - 1, 0), 0)),
                  pl.BlockSpec((1, POOL_HALO, BRANCH),
                               lambda b, i: (b, jnp.minimum((i + 1) * hb, nhb - 1), 0)),
                  pl.BlockSpec((1, 1, D), mod_map),
                  pl.BlockSpec((BRANCH, BRANCH), full2), row(BRANCH),
                  pl.BlockSpec((BRANCH, BRANCH), full2), row(BRANCH),
                  row(BRANCH), row(BRANCH),
                  pl.BlockSpec((SGU_HEADS, CHUNK, CHUNK), lambda b, i: (0, 0, 0)),
                  pl.BlockSpec((CHUNK, BRANCH), full2),
                  pl.BlockSpec((4 * BRANCH, D), full2), row(D), row(D)],
        out_specs=pl.BlockSpec((1, tm, D), lambda b, i: (b, i, 0)),
        scratch_shapes=[pltpu.VMEM((tm + 2 * POOL_HALO, BRANCH), F32) for _ in range(4)],
        compiler_params=pltpu.CompilerParams(
            dimension_semantics=("parallel", "parallel"), vmem_limit_bytes=V7X_VMEM_LIMIT),
        name="merge",
    )(x, att, y5, mix, mix, mix, gate, wpool, pscale, wglu, bglu, sg, sb, ws, bs, wout, lng, lnb)


def _prep_inproj_weights(w_in, g_q, w_uq, g_kv, w_ukv):
    D = w_in.shape[0]
    o_kr = KV_LORA
    o_s5 = o_kr + ROPE
    o_cq = o_s5 + BRANCH
    o_rest = o_cq + Q_LORA
    wb = w_in.astype(BF16)
    zpad = lambda k: jnp.zeros((D, k), BF16)
    w = jnp.concatenate(
        [wb[:, 0:KV_LORA], wb[:, o_cq:o_cq + Q_LORA], zpad(C_KR - C_CQ - Q_LORA),
         wb[:, o_kr:o_kr + ROPE], zpad(C_S5 - C_KR - ROPE), wb[:, o_s5:o_s5 + BRANCH],
         wb[:, o_rest:]], axis=1)
    assert w.shape == (D, IN_PAD)
    hw = N_HEADS * HEAD_PAD
    scale = (NOPE + ROPE) ** -0.5 * math.log2(math.e)
    ukv = w_ukv.reshape(KV_LORA, N_HEADS, NOPE + V_DIM)
    uq = w_uq.reshape(Q_LORA, N_HEADS, NOPE + ROPE) * scale
    padl = lambda a: jnp.pad(a, ((0, 0), (0, 0), (0, HEAD_PAD - a.shape[2])))
    wk = padl(ukv[:, :, :NOPE])
    wv = padl(ukv[:, :, NOPE:])
    wq = padl(uq)
    pk = (jnp.arange(LANE)[:, None] + NOPE == jnp.arange(LANE)[None, :]) & (jnp.arange(LANE)[:, None] < ROPE)
    vone = jnp.tile(jnp.arange(HEAD_PAD) == V_DIM, N_HEADS).astype(F32)
    return (w, g_kv.reshape(1, KV_LORA), g_q.reshape(1, Q_LORA),
            wk.reshape(KV_LORA, hw).astype(BF16), wv.reshape(KV_LORA, hw).astype(BF16),
            wq.reshape(Q_LORA, hw).astype(BF16), pk.astype(BF16), vone.reshape(1, hw))


def _rope_tables(n, rotate):
    ones = jnp.ones((n, NOPE), F32)
    zpad = jnp.zeros((n, HEAD_PAD - NOPE - ROPE), F32)
    if not rotate:
        c = jnp.concatenate([ones, jnp.ones((n, ROPE), F32), zpad], axis=1)
        z = jnp.zeros((n, HEAD_PAD), F32)
        return c, z, z
    nf = ROPE // 4
    rows = n // GRID_W
    lane = np.arange(HEAD_PAD)
    blk = (lane - NOPE) // (2 * nf)
    in_rope = (lane >= NOPE) & (lane < NOPE + ROPE)
    first = ((lane - NOPE) % (2 * nf)) < nf
    inv = ROPE_BASE ** (-jnp.arange(nf, dtype=F32) / nf)
    inv_lane = jnp.tile(inv, HEAD_PAD // nf)
    is_row = jnp.asarray(in_rope & (blk == 0))
    is_col = jnp.asarray(in_rope & (blk == 1))
    ang_r = jnp.arange(rows, dtype=F32)[:, None] * inv_lane[None, :]
    ang_c = jnp.arange(GRID_W, dtype=F32)[:, None] * inv_lane[None, :]

    def grid(fr, fc, base):
        t3 = (jnp.where(is_row, fr, 0.0)[:, None, :] + jnp.where(is_col, fc, 0.0)[None, :, :]
              + base[None, None, :])
        return t3.reshape(n, HEAD_PAD)

    nope_one = jnp.asarray((lane < NOPE).astype(np.float32))
    zero = jnp.zeros((HEAD_PAD,), F32)
    m1 = jnp.asarray(first)
    c = grid(jnp.cos(ang_r), jnp.cos(ang_c), nope_one)
    s1 = grid(jnp.where(m1, -jnp.sin(ang_r), 0.0), jnp.where(m1, -jnp.sin(ang_c), 0.0), zero)
    s2 = grid(jnp.where(m1, 0.0, jnp.sin(ang_r)), jnp.where(m1, 0.0, jnp.sin(ang_c)), zero)
    return c, s1, s2


def _s5_toeplitz_selector():
    T = S5_T
    sel = np.zeros((T, T, 2 * T + 1), np.float32)
    for t in range(T):
        for s in range(T):
            if t >= s:
                sel[t, s, t - s] = 1.0
            if s >= t:
                sel[t, s, T + s - t] = 1.0
        sel[t, t, 2 * T] = 1.0
    return sel


def _prep_s5(lam_re, lam_im, log_dt, b_re, b_im, c_re, c_im, s5_d):
    T, G, P, H = S5_T, S5_G, S5_P, S5_H
    hi = lax.Precision.HIGHEST
    lam_re = lam_re.astype(F32)
    lam_im = lam_im.astype(F32)
    dt = jnp.exp(log_dt.astype(F32))[..., None]
    zr, zi = lam_re * dt, lam_im * dt
    j = jnp.arange(T + 1, dtype=F32)[:, None, None, None]
    mag = jnp.exp(zr * j)
    pr, pi = mag * jnp.cos(zi * j), mag * jnp.sin(zi * j)
    nr, ni = pr[1] - 1.0, pi[1]
    den = lam_re * lam_re + lam_im * lam_im
    k_re = (nr * lam_re + ni * lam_im) / den
    k_im = (ni * lam_re - nr * lam_im) / den
    bb_re = k_re[..., None] * b_re - k_im[..., None] * b_im
    bb_im = k_re[..., None] * b_im + k_im[..., None] * b_re
    cl_re = c_re[None] * pr[:, :, :, None, :] - c_im[None] * pi[:, :, :, None, :]
    cl_im = c_re[None] * pi[:, :, :, None, :] + c_im[None] * pr[:, :, :, None, :]
    kk = (jnp.einsum('jdgop,dgph->jdgoh', cl_re[:T], bb_re, precision=hi)
          - jnp.einsum('jdgop,dgph->jdgoh', cl_im[:T], bb_im, precision=hi))
    skip = jnp.eye(H, dtype=F32)[None] * s5_d.astype(F32).reshape(G, H, 1)
    k2 = jnp.concatenate([kk[:, 0], kk[:, 1], skip[None]], axis=0)
    m = jnp.einsum('tsj,jgoh->gtosh', jnp.asarray(_s5_toeplitz_selector()), k2,
                   precision=hi).reshape(G, T * H, T * H)
    tt = jnp.arange(T)
    over_s = lambda a: jnp.repeat(a.transpose(1, 2, 0), H, axis=2)
    over_h = lambda a: jnp.tile(a, (1, 1, T))

    def lb(p_r, p_i, d):
        ar, ai, br, bi = over_s(p_r), over_s(p_i), over_h(bb_re[d]), over_h(bb_im[d])
        return ar * br - ai * bi, ar * bi + ai * br

    f_re, f_im = lb(pr[T - 1 - tt, 0], pi[T - 1 - tt, 0], 0)
    b_re_, b_im_ = lb(pr[tt, 1], pi[tt, 1], 1)
    pb = jnp.concatenate([f_re, b_re_, f_im, b_im_], axis=1)
    to_c = lambda a: a.transpose(1, 0, 2, 3).reshape(G, T * H, P)
    zf_re, zf_im = cl_re[1 + tt, 0], cl_im[1 + tt, 0]
    zb_re, zb_im = cl_re[T - tt, 1], cl_im[T - tt, 1]
    pc = jnp.concatenate([to_c(zf_re), to_c(zb_re), -to_c(zf_im), -to_c(zb_im)], axis=2)
    lanes = lambda a: jnp.concatenate([a[0], a[1]], axis=-1).reshape(1, G * 2 * P)
    return (m.astype(BF16), pb.astype(BF16), pc.astype(BF16), lanes(pr[T]), lanes(pi[T]))


def _prep_merge_weights(w_pool, pool_scale, w_glu, b_glu, sgu_g, sgu_b, w_s, b_s, w_out, ln_g, ln_b):
    z = jnp.zeros((POOL_GC, POOL_GC), F32)
    ng = len(POOL_WINDOWS)
    wpool = jnp.concatenate(
        [jnp.concatenate([w_pool[gi] if gj == gi else z for gj in range(ng)], axis=1)
         for gi in range(ng)], axis=0)
    bs = jnp.repeat(b_s.T, SGU_HD, axis=1)
    r = lambda a: a.reshape(1, -1).astype(F32)
    return (wpool.astype(BF16), r(pool_scale), w_glu.astype(BF16), r(b_glu), r(sgu_g), r(sgu_b),
            w_s.astype(BF16), bs.astype(F32), w_out.astype(BF16), r(ln_g), r(ln_b))


def kernel(x, c, ctx, c_ctx, w_mod, b_mod, w_in, g_q, w_uq, g_kv, w_ukv, w_pool, pool_scale, lam_re, lam_im, log_dt, s5_b_re, s5_b_im, s5_c_re, s5_c_im, s5_d, w_glu, b_glu, sgu_g, sgu_b, w_s, b_s, w_out, ln_g, ln_b):
    B, n, D = x.shape
    nc = ctx.shape[1]
    depth = w_mod.shape[0]
    alpha = (2 * depth) ** 0.25
    assert B + 1 <= 8
    tm = min(ATT_TK, n)
    tmc = min(ATT_TK, nc)
    tq = min(512, n)
    tqc = min(512, nc)

    cc = jnp.concatenate([c.astype(F32), c_ctx.astype(F32).reshape(1, D),
                          jnp.zeros((8 - B - 1, D), F32)], axis=0)
    mod = _modulation(cc, w_mod, b_mod.reshape(depth, 1, 3 * D))
    tab_l = _rope_tables(n, True)
    tab_c = _rope_tables(nc, False)

    wts_in_all = jax.vmap(_prep_inproj_weights)(w_in, g_q, w_uq, g_kv, w_ukv)
    s5_ops_all = jax.vmap(_prep_s5)(lam_re, lam_im, log_dt, s5_b_re, s5_b_im, s5_c_re, s5_c_im, s5_d)
    wts_mg_all = jax.vmap(_prep_merge_weights)(w_pool, pool_scale, w_glu, b_glu, sgu_g, sgu_b,
                                               w_s, b_s, w_out, ln_g, ln_b)

    for l in range(depth):
        last = l == depth - 1
        shift = mod[l, :, 0:D].reshape(8, 1, D)
        scale = mod[l, :, D:2 * D].reshape(8, 1, D)
        gate = mod[l, :, 2 * D:].reshape(8, 1, D)
        wts_in = tuple(a[l] for a in wts_in_all)
        s5_ops = tuple(a[l] for a in s5_ops_all)
        wts_mg = tuple(a[l] for a in wts_mg_all)

        q_l, k_l, v_l, s5u_l, mix_l = _inproj(x, shift[:B], scale[:B], True, wts_in, tab_l, tm)
        q_c, k_c, v_c, s5u_c, mix_c = _inproj(ctx, shift[B:B + 1], scale[B:B + 1], False, wts_in, tab_c, tmc)
        att_l = _attention(q_l, [k_l, k_c], [v_l, v_c], tq)
        y5_l, y5_c = _s5(s5u_l, s5u_c, s5_ops)
        x_new = _merge(x, att_l, y5_l, mix_l, gate[:B], True, wts_mg, tm, alpha)
        if not last:
            att_c = _attention(q_c, [k_c], [v_c], tqc)
            ctx = _merge(ctx, att_c, y5_c, mix_c, gate[B:B + 1], False, wts_mg, tmc, alpha)
        x = x_new
    return x
```

```python
import functools
import math

import jax
import jax.numpy as jnp
import numpy as np
from jax import lax
from jax.experimental import pallas as pl
from jax.experimental.pallas import tpu as pltpu

F32 = jnp.float32
BF16 = jnp.bfloat16

GRID_W = 64
N_HEADS = 4
NOPE = 64
ROPE = 32
V_DIM = 64
Q_LORA = 192
KV_LORA = 128
ROPE_BASE = 10000.0
BRANCH = 256
POOL_WINDOWS = (2, 4, 8, 16)
POOL_GC = 64
S5_H = 16
S5_G = 16
S5_P = 64
SGU_HEADS = 4
SGU_HD = 64
CHUNK = 128
LN_EPS = 1e-6

LANE = 128
HEAD_PAD = 128
V_ROWS = 80
V7X_VMEM_LIMIT = 56 * 1024 * 1024
S5_T = 16
S5_BLK = S5_T * LANE

C_CKV = 0
C_CQ = 128
C_KR = 384
C_S5 = 512
C_MIX = 768
IN_PAD = 2560
MIX_W = 3 * BRANCH + 4 * BRANCH


def _dot(a, b):
    return jnp.dot(a, b, preferred_element_type=F32)


def _dot_nt(a, b):
    return lax.dot_general(a, b, (((1,), (1,)), ((), ())), preferred_element_type=F32)


def _ln_rows(x):
    mu = jnp.mean(x, axis=-1, keepdims=True)
    xc = x - mu
    var = jnp.mean(xc * xc, axis=-1, keepdims=True)
    return xc * lax.rsqrt(var + LN_EPS)


def _sigmoid(x):
    return 0.5 * jnp.tanh(0.5 * x) + 0.5


def _mod_kernel(c_ref, w_ref, b_ref, o_ref):
    c = c_ref[...]
    s = c * _sigmoid(c)
    o_ref[0] = jnp.dot(s, w_ref[0], preferred_element_type=F32,
                       precision=lax.Precision.HIGHEST) + b_ref[0]


def _modulation(cc, w_mod, b_mod):
    L, D, D3 = w_mod.shape
    nb = D3 // D
    return pl.pallas_call(
        _mod_kernel,
        out_shape=jax.ShapeDtypeStruct((L, 8, D3), F32),
        grid=(L, nb),
        in_specs=[pl.BlockSpec((8, D), lambda l, j: (0, 0)),
                  pl.BlockSpec((1, D, D), lambda l, j: (l, 0, j)),
                  pl.BlockSpec((1, 1, D), lambda l, j: (l, 0, j))],
        out_specs=pl.BlockSpec((1, 8, D), lambda l, j: (l, 0, j)),
        compiler_params=pltpu.CompilerParams(vmem_limit_bytes=V7X_VMEM_LIMIT),
        name="modulation",
    )(cc, w_mod, b_mod)


def _rope_lanes(x, c, s1, s2):
    w = x.shape[-1]
    return x * c + pltpu.roll(x, w - 8, 1) * s1 + pltpu.roll(x, 8, 1) * s2


def _inproj_kernel(x_ref, shift_ref, scale_ref, w_ref, gkv_ref, gq_ref, wk_ref, wv_ref, wq_ref,
                   pk_ref, vone_ref, rc_ref, rs1_ref, rs2_ref,
                   q_ref, k_ref, vt_ref, s5_ref, mix_ref):
    x = x_ref[0]
    h = _ln_rows(x) * (1.0 + scale_ref[0]) + shift_ref[0]
    proj = _dot(h.astype(BF16), w_ref[...])
    ckv = proj[:, C_CKV:C_CKV + KV_LORA]
    cq = proj[:, C_CQ:C_CQ + Q_LORA]
    kr = proj[:, C_KR:C_KR + LANE]

    rk = ckv * lax.rsqrt(jnp.mean(ckv * ckv, axis=-1, keepdims=True) + LN_EPS) * gkv_ref[...]
    rk = rk.astype(BF16)
    rq = cq * lax.rsqrt(jnp.mean(cq * cq, axis=-1, keepdims=True) + LN_EPS) * gq_ref[...]
    rq = rq.astype(BF16)

    rc, rs1, rs2 = rc_ref[...], rs1_ref[...], rs2_ref[...]
    krp = _rope_lanes(_dot(kr.astype(BF16), pk_ref[...]), rc, rs1, rs2)
    kn = _dot(rk, wk_ref[...])
    vv = _dot(rk, wv_ref[...]) + vone_ref[...]
    qq = _dot(rq, wq_ref[...])
    for hd in range(N_HEADS):
        sl = slice(hd * HEAD_PAD, (hd + 1) * HEAD_PAD)
        q_ref[0, hd] = _rope_lanes(qq[:, sl], rc, rs1, rs2).astype(BF16)
        k_ref[0, hd] = (kn[:, sl] + krp).astype(BF16)
        vt_ref[0, hd, 0] = vv[:, sl].T[:V_ROWS].astype(BF16)
    s5_ref[0] = proj[:, C_S5:C_S5 + BRANCH].astype(BF16)
    mix_ref[0] = proj[:, C_MIX:C_MIX + MIX_W].astype(BF16)


def _inproj(x, shift, scale, batch_mod, wts, tables, tm):
    B, n, D = x.shape
    w_in, gkv, gq, wk, wv, wq, pk, vone = wts
    rc, rs1, rs2 = tables
    mod_map = (lambda b, i: (b, 0, 0)) if batch_mod else (lambda b, i: (0, 0, 0))
    full2 = lambda b, i: (0, 0)
    tab = pl.BlockSpec((tm, LANE), lambda b, i: (i, 0))
    hw = N_HEADS * HEAD_PAD
    return pl.pallas_call(
        _inproj_kernel,
        out_shape=(jax.ShapeDtypeStruct((B, N_HEADS, n, HEAD_PAD), BF16),
                   jax.ShapeDtypeStruct((B, N_HEADS, n, HEAD_PAD), BF16),
                   jax.ShapeDtypeStruct((B, N_HEADS, n // tm, V_ROWS, tm), BF16),
                   jax.ShapeDtypeStruct((B, n, BRANCH), BF16),
                   jax.ShapeDtypeStruct((B, n, MIX_W), BF16)),
        grid=(B, n // tm),
        in_specs=[pl.BlockSpec((1, tm, D), lambda b, i: (b, i, 0)),
                  pl.BlockSpec((1, 1, D), mod_map),
                  pl.BlockSpec((1, 1, D), mod_map),
                  pl.BlockSpec((D, IN_PAD), full2),
                  pl.BlockSpec((1, KV_LORA), full2),
                  pl.BlockSpec((1, Q_LORA), full2),
                  pl.BlockSpec((KV_LORA, hw), full2),
                  pl.BlockSpec((KV_LORA, hw), full2),
                  pl.BlockSpec((Q_LORA, hw), full2),
                  pl.BlockSpec((LANE, LANE), full2),
                  pl.BlockSpec((1, hw), full2),
                  tab, tab, tab],
        out_specs=(pl.BlockSpec((1, N_HEADS, tm, HEAD_PAD), lambda b, i: (b, 0, i, 0)),
                   pl.BlockSpec((1, N_HEADS, tm, HEAD_PAD), lambda b, i: (b, 0, i, 0)),
                   pl.BlockSpec((1, N_HEADS, 1, V_ROWS, tm), lambda b, i: (b, 0, i, 0, 0)),
                   pl.BlockSpec((1, tm, BRANCH), lambda b, i: (b, i, 0)),
                   pl.BlockSpec((1, tm, MIX_W), lambda b, i: (b, i, 0))),
        compiler_params=pltpu.CompilerParams(
            dimension_semantics=("parallel", "parallel"), vmem_limit_bytes=V7X_VMEM_LIMIT),
        name="inproj",
    )(x, shift, scale, w_in, gkv, gq, wk, wv, wq, pk, vone, rc, rs1, rs2)


ATT_TK = 512


def _attn_kernel(*refs, n_src):
    q_ref = refs[0]
    k_refs = refs[1:1 + n_src]
    vt_refs = refs[1 + n_src:1 + 2 * n_src]
    o_ref = refs[1 + 2 * n_src]
    s_scr, pa_scr, pb_scr, acc_scr, m_scr = refs[2 + 2 * n_src:]
    tq = q_ref.shape[2]
    step = pl.program_id(1)

    @pl.when(step == 0)
    def _():
        s_scr[...] = jnp.zeros(s_scr.shape, F32)
        m_scr[...] = jnp.zeros(m_scr.shape, F32)

    q = q_ref[0, 0]
    m_prev = jnp.max(m_scr[...], axis=0, keepdims=True)
    acc_scr[...] = jnp.zeros(acc_scr.shape, F32)
    m_acc = jnp.full((8, tq), -jnp.inf, F32)

    tiles = []
    row = 0
    for si in range(n_src):
        nt, tk = vt_refs[si].shape[2], vt_refs[si].shape[4]
        for j in range(nt):
            tiles.append((si, j, row, tk))
            row += tk
    p_bufs = ((pa_scr, 0), (pa_scr, 1), (pb_scr, 0), (pb_scr, 1))

    def ex(i):
        si, j, r, tk = tiles[i]
        buf, slot = p_bufs[i % len(p_bufs)]
        buf[slot, 0:tk, :] = jnp.exp2(s_scr[r:r + tk, :] - m_prev).astype(BF16)

    def qk(i, m):
        si, j, r, tk = tiles[i]
        s = _dot_nt(k_refs[si][0, 0, j * tk:(j + 1) * tk, :], q)
        s_scr[r:r + tk, :] = s
        parts = [s[a:a + 8, :] for a in range(0, tk, 8)]
        while len(parts) > 1:
            parts = [jnp.maximum(parts[a], parts[a + 1]) for a in range(0, len(parts), 2)]
        return jnp.maximum(m, parts[0])

    def mm(i):
        si, j, r, tk = tiles[i]
        buf, slot = p_bufs[i % len(p_bufs)]
        acc_scr[...] += _dot(vt_refs[si][0, 0, j], buf[slot, 0:tk, :])

    ex(0)
    for i in range(len(tiles)):
        if i + 1 < len(tiles):
            ex(i + 1)
        m_acc = qk(i, m_acc)
        mm(i)

    m_scr[...] = m_acc

    @pl.when(step > 0)
    def _():
        acc = acc_scr[...]
        o_ref[0, 0] = (acc[:V_DIM, :] / acc[V_DIM:V_DIM + 1, :]).astype(o_ref.dtype)


def _attention(q, ks, vts, tq):
    B, H, n, _ = q.shape
    n_src = len(ks)
    nkeys = sum(k.shape[2] for k in ks)
    tk_max = max(v.shape[4] for v in vts)
    for k, v in zip(ks, vts):
        nt = v.shape[2]
        assert k.shape[2] == nt * v.shape[4]
    nq = n // tq
    n_items = H * nq
    cur = lambda k: jnp.minimum(k, n_items - 1)
    prv = lambda k: jnp.maximum(k - 1, 0)
    k_spec = lambda a: pl.BlockSpec((1, 1) + a.shape[2:], lambda b, k: (b, cur(k) // nq, 0, 0))
    vt_spec = lambda a: pl.BlockSpec((1, 1) + a.shape[2:], lambda b, k: (b, prv(k) // nq, 0, 0, 0))
    return pl.pallas_call(
        functools.partial(_attn_kernel, n_src=n_src),
        out_shape=jax.ShapeDtypeStruct((B, H, V_DIM, n), BF16),
        grid=(B, n_items + 1),
        in_specs=[pl.BlockSpec((1, 1, tq, HEAD_PAD), lambda b, k: (b, cur(k) // nq, cur(k) % nq, 0))]
                 + [k_spec(k) for k in ks] + [vt_spec(v) for v in vts],
        out_specs=pl.BlockSpec((1, 1, V_DIM, tq), lambda b, k: (b, prv(k) // nq, 0, prv(k) % nq)),
        scratch_shapes=[pltpu.VMEM((nkeys, tq), F32),
                        pltpu.VMEM((2, tk_max, tq), BF16), pltpu.VMEM((2, tk_max, tq), BF16),
                        pltpu.VMEM((V_ROWS, tq), F32), pltpu.VMEM((8, tq), F32)],
        compiler_params=pltpu.CompilerParams(
            dimension_semantics=("parallel", "arbitrary"), vmem_limit_bytes=V7X_VMEM_LIMIT),
        name="attention",
    )(q, *ks, *vts)


def _s5_kernel(ul_ref, uc_ref, m_ref, pb_ref, pc_ref, are_ref, aim_ref, yl_ref, yc_ref,
               ut_ref, sre_ref, sim_ref, f_ref):
    n = ul_ref.shape[1]
    nc = uc_ref.shape[1]
    kl = n // S5_T
    kc = nc // S5_T
    nblk = n // S5_BLK
    rows_blk = S5_BLK // S5_T
    ncol = ut_ref.shape[2]

    half_g = S5_G // 2

    def relayout_in(nrows, col0):
        for s in range(S5_T):
            for h in range(2):
                blk = f_ref[h, pl.ds(s, nrows, stride=S5_T), :]
                if nrows < LANE:
                    blk = jnp.concatenate([blk, jnp.zeros((LANE - nrows, LANE), F32)], axis=0)
                bt = blk.T
                for gg in range(half_g):
                    ut_ref[h * half_g + gg, s * S5_H:(s + 1) * S5_H, col0:col0 + LANE] = (
                        bt[gg * S5_H:(gg + 1) * S5_H, :].astype(BF16))

    for jb in range(nblk):
        for h in range(2):
            f_ref[h] = ul_ref[0, jb * S5_BLK:(jb + 1) * S5_BLK, h * LANE:(h + 1) * LANE].astype(F32)
        relayout_in(rows_blk, jb * LANE)
    for h in range(2):
        f_ref[h, 0:nc, :] = uc_ref[0, :, h * LANE:(h + 1) * LANE].astype(F32)
    relayout_in(kc, kl)

    for g in range(S5_G):
        st = _dot(pb_ref[g], ut_ref[g])
        stt = st.T
        sre_ref[:, g * LANE:(g + 1) * LANE] = stt[:, :LANE]
        sim_ref[:, g * LANE:(g + 1) * LANE] = stt[:, LANE:]

    are = are_ref[...]
    aim = aim_ref[...]
    lane = lax.broadcasted_iota(jnp.int32, (1, S5_G * LANE), 1)
    is_f = (lane % LANE) < S5_P

    def step(rf, rb, carry):
        hr, hi = carry
        srf, sif = sre_ref[pl.ds(rf, 1), :], sim_ref[pl.ds(rf, 1), :]
        srb, sib = sre_ref[pl.ds(rb, 1), :], sim_ref[pl.ds(rb, 1), :]
        sre_ref[pl.ds(rf, 1), :] = jnp.where(is_f, hr, srf)
        sim_ref[pl.ds(rf, 1), :] = jnp.where(is_f, hi, sif)
        sre_ref[pl.ds(rb, 1), :] = jnp.where(is_f, srb, hr)
        sim_ref[pl.ds(rb, 1), :] = jnp.where(is_f, sib, hi)
        sr = jnp.where(is_f, srf, srb)
        si = jnp.where(is_f, sif, sib)
        return (are * hr - aim * hi + sr, are * hi + aim * hr + si)

    zero = jnp.zeros((1, S5_G * LANE), F32)
    carry = lax.fori_loop(0, kc, lambda i, c: step(kl + i, kl + kc - 1 - i, c), (zero, zero))
    lax.fori_loop(0, kl, lambda i, c: step(i, kl - 1 - i, c), carry)

    for g in range(S5_G):
        hin = jnp.concatenate([sre_ref[:, g * LANE:(g + 1) * LANE],
                               sim_ref[:, g * LANE:(g + 1) * LANE]], axis=1)
        hin_t = hin.T.astype(BF16)
        y = _dot(m_ref[g], ut_ref[g]) + _dot(pc_ref[g], hin_t)
        ut_ref[g] = y.astype(BF16)

    def relayout_out(nrows, col0):
        for t in range(S5_T):
            for h in range(2):
                yt = jnp.concatenate(
                    [ut_ref[h * half_g + gg, t * S5_H:(t + 1) * S5_H, col0:col0 + LANE]
                     for gg in range(half_g)], axis=0).astype(F32)
                f_ref[h, pl.ds(t, nrows, stride=S5_T), :] = yt.T[:nrows]

    for jb in range(nblk):
        relayout_out(rows_blk, jb * LANE)
        for h in range(2):
            yl_ref[0, jb * S5_BLK:(jb + 1) * S5_BLK, h * LANE:(h + 1) * LANE] = (
                f_ref[h].astype(yl_ref.dtype))
    relayout_out(kc, kl)
    for h in range(2):
        yc_ref[0, :, h * LANE:(h + 1) * LANE] = f_ref[h, 0:nc, :].astype(yc_ref.dtype)


def _s5(ul, uc, ops):
    B, n, _ = ul.shape
    nc = uc.shape[1]
    m, pb, pc, are, aim = ops
    assert n % S5_BLK == 0 and nc % S5_T == 0 and nc // S5_T <= LANE and nc <= S5_BLK
    assert (n // S5_T) % 2 == 0 and (nc // S5_T) % 2 == 0
    ncol = n // S5_T + LANE
    w3 = lambda b: (0, 0, 0)
    return pl.pallas_call(
        _s5_kernel,
        out_shape=(jax.ShapeDtypeStruct((B, n, BRANCH), BF16),
                   jax.ShapeDtypeStruct((B, nc, BRANCH), BF16)),
        grid=(B,),
        in_specs=[pl.BlockSpec((1, n, BRANCH), lambda b: (b, 0, 0)),
                  pl.BlockSpec((1, nc, BRANCH), lambda b: (b, 0, 0)),
                  pl.BlockSpec(m.shape, w3), pl.BlockSpec(pb.shape, w3), pl.BlockSpec(pc.shape, w3),
                  pl.BlockSpec(are.shape, lambda b: (0, 0)), pl.BlockSpec(aim.shape, lambda b: (0, 0))],
        out_specs=(pl.BlockSpec((1, n, BRANCH), lambda b: (b, 0, 0)),
                   pl.BlockSpec((1, nc, BRANCH), lambda b: (b, 0, 0))),
        scratch_shapes=[pltpu.VMEM((S5_G, S5_T * S5_H, ncol), BF16),
                        pltpu.VMEM((ncol, S5_G * LANE), F32),
                        pltpu.VMEM((ncol, S5_G * LANE), F32),
                        pltpu.VMEM((2, S5_BLK, LANE), F32)],
        compiler_params=pltpu.CompilerParams(
            dimension_semantics=("parallel",), vmem_limit_bytes=V7X_VMEM_LIMIT),
        name="s5",
    )(ul, uc, m, pb, pc, are, aim)


POOL_HALO = 16


def _gelu_tanh(x):
    return 0.5 * x * (1.0 + jnp.tanh(math.sqrt(2.0 / math.pi) * (x + 0.044715 * (x * x * x))))


def _merge_kernel(x_ref, att_ref, y5_ref, mix_ref, prev_ref, next_ref, gate_ref,
                  wpool_ref, pscale_ref, wglu_ref, bglu_ref, sg_ref, sb_ref, ws_ref, bs_ref,
                  wout_ref, lng_ref, lnb_ref, o_ref, a_ref, b2_ref, b4_ref, b8_ref,
                  *, n_total, alpha):
    tm = x_ref.shape[1]
    i = pl.program_id(1)
    ni = pl.num_programs(1)
    hl = POOL_HALO

    pool_in = mix_ref[0, :, 0:BRANCH].astype(F32)
    a_ref[0:hl, :] = jnp.where(i > 0, prev_ref[0].astype(F32), 0.0)
    a_ref[hl:hl + tm, :] = pool_in
    a_ref[hl + tm:hl + tm + hl, :] = jnp.where(i < ni - 1, next_ref[0].astype(F32), 0.0)
    r2 = tm + 16
    b2_ref[8:8 + r2, :] = a_ref[7:7 + r2, :] + a_ref[8:8 + r2, :]
    r4 = tm + 12
    b4_ref[10:10 + r4, :] = b2_ref[9:9 + r4, :] + b2_ref[11:11 + r4, :]
    r8 = tm + 8
    b8_ref[12:12 + r8, :] = b4_ref[10:10 + r8, :] + b4_ref[14:14 + r8, :]
    c2 = b2_ref[hl:hl + tm, :]
    c4 = b4_ref[hl:hl + tm, :]
    c8 = b8_ref[hl:hl + tm, :]
    c16 = b8_ref[hl - 4:hl - 4 + tm, :] + b8_ref[hl + 4:hl + 4 + tm, :]
    lane = lax.broadcasted_iota(jnp.int32, (tm, BRANCH), 1)
    grp = lane // POOL_GC
    tot = jnp.where(grp == 0, c2, jnp.where(grp == 1, c4, jnp.where(grp == 2, c8, c16)))
    t = lax.broadcasted_iota(jnp.int32, (tm, BRANCH), 0) + i * tm
    half = jnp.left_shift(1, grp)
    cnt = jnp.minimum(t + half, n_total) - jnp.maximum(t - half, 0)
    pooled = tot / cnt.astype(F32) - pool_in
    pool_l = _dot(pooled.astype(BF16), wpool_ref[...]) * pscale_ref[...]

    g1 = _gelu_tanh(y5_ref[0].astype(F32))
    s5_l = g1 * _sigmoid(_dot(g1.astype(BF16), wglu_ref[...]) + bglu_ref[...])

    su = mix_ref[0, :, BRANCH:2 * BRANCH].astype(F32)
    sv = mix_ref[0, :, 2 * BRANCH:3 * BRANCH].astype(F32)
    vn = (_ln_rows(sv) * sg_ref[...] + sb_ref[...]).astype(BF16)
    lane_c = lax.broadcasted_iota(jnp.int32, (CHUNK, BRANCH), 1) // SGU_HD
    mixed = []
    for c in range(tm // CHUNK):
        vc = vn[c * CHUNK:(c + 1) * CHUNK, :]
        r = _dot(ws_ref[SGU_HEADS - 1], vc)
        for hd in range(SGU_HEADS - 2, -1, -1):
            r = jnp.where(lane_c == hd, _dot(ws_ref[hd], vc), r)
        mixed.append(r + bs_ref[...])
    sgu_l = su * jnp.concatenate(mixed, axis=0)

    gts = mix_ref[0, :, 3 * BRANCH:].astype(F32)
    att = att_ref[0].astype(F32).reshape(N_HEADS * V_DIM, tm).T
    cat = jnp.concatenate([att, pool_l, s5_l, sgu_l], axis=-1)
    y = _dot((cat * (gts * _sigmoid(gts))).astype(BF16), wout_ref[...])
    z = alpha * x_ref[0] + gate_ref[0] * y
    o_ref[0] = _ln_rows(z) * lng_ref[...] + lnb_ref[...]


def _merge(x, att, y5, mix, gate, batch_mod, wts, tm, alpha):
    B, n, D = x.shape
    wpool, pscale, wglu, bglu, sg, sb, ws, bs, wout, lng, lnb = wts
    hb = tm // POOL_HALO
    nhb = n // POOL_HALO
    full2 = lambda b, i: (0, 0)
    mod_map = (lambda b, i: (b, 0, 0)) if batch_mod else (lambda b, i: (0, 0, 0))
    row = lambda w: pl.BlockSpec((1, w), full2)
    return pl.pallas_call(
        functools.partial(_merge_kernel, n_total=n, alpha=alpha),
        out_shape=jax.ShapeDtypeStruct((B, n, D), F32),
        grid=(B, n // tm),
        in_specs=[pl.BlockSpec((1, tm, D), lambda b, i: (b, i, 0)),
                  pl.BlockSpec((1, N_HEADS, V_DIM, tm), lambda b, i: (b, 0, 0, i)),
                  pl.BlockSpec((1, tm, BRANCH), lambda b, i: (b, i, 0)),
                  pl.BlockSpec((1, tm, MIX_W), lambda b, i: (b, i, 0)),
                  pl.BlockSpec((1, POOL_HALO, BRANCH),
                               lambda b, i: (b, jnp.maximum(i * hb - 1, 0), 0)),
                  pl.BlockSpec((1, POOL_HALO, BRANCH),
                               lambda b, i: (b, jnp.minimum((i + 1) * hb, nhb - 1), 0)),
                  pl.BlockSpec((1, 1, D), mod_map),
                  pl.BlockSpec((BRANCH, BRANCH), full2), row(BRANCH),
                  pl.BlockSpec((BRANCH, BRANCH), full2), row(BRANCH),
                  row(BRANCH), row(BRANCH),
                  pl.BlockSpec((SGU_HEADS, CHUNK, CHUNK), lambda b, i: (0, 0, 0)),
                  pl.BlockSpec((CHUNK, BRANCH), full2),
                  pl.BlockSpec((4 * BRANCH, D), full2), row(D), row(D)],
        out_specs=pl.BlockSpec((1, tm, D), lambda b, i: (b, i, 0)),
        scratch_shapes=[pltpu.VMEM((tm + 2 * POOL_HALO, BRANCH), F32) for _ in range(4)],
        compiler_params=pltpu.CompilerParams(
            dimension_semantics=("parallel", "parallel"), vmem_limit_bytes=V7X_VMEM_LIMIT),
        name="merge",
    )(x, att, y5, mix, mix, mix, gate, wpool, pscale, wglu, bglu, sg, sb, ws, bs, wout, lng, lnb)


def _prep_inproj_weights(w_in, g_q, w_uq, g_kv, w_ukv):
    D = w_in.shape[0]
    o_kr = KV_LORA
    o_s5 = o_kr + ROPE
    o_cq = o_s5 + BRANCH
    o_rest = o_cq + Q_LORA
    wb = w_in.astype(BF16)
    zpad = lambda k: jnp.zeros((D, k), BF16)
    w = jnp.concatenate(
        [wb[:, 0:KV_LORA], wb[:, o_cq:o_cq + Q_LORA], zpad(C_KR - C_CQ - Q_LORA),
         wb[:, o_kr:o_kr + ROPE], zpad(C_S5 - C_KR - ROPE), wb[:, o_s5:o_s5 + BRANCH],
         wb[:, o_rest:]], axis=1)
    assert w.shape == (D, IN_PAD)
    hw = N_HEADS * HEAD_PAD
    scale = (NOPE + ROPE) ** -0.5 * math.log2(math.e)
    ukv = w_ukv.reshape(KV_LORA, N_HEADS, NOPE + V_DIM)
    uq = w_uq.reshape(Q_LORA, N_HEADS, NOPE + ROPE) * scale
    padl = lambda a: jnp.pad(a, ((0, 0), (0, 0), (0, HEAD_PAD - a.shape[2])))
    wk = padl(ukv[:, :, :NOPE])
    wv = padl(ukv[:, :, NOPE:])
    wq = padl(uq)
    pk = (jnp.arange(LANE)[:, None] + NOPE == jnp.arange(LANE)[None, :]) & (jnp.arange(LANE)[:, None] < ROPE)
    vone = jnp.tile(jnp.arange(HEAD_PAD) == V_DIM, N_HEADS).astype(F32)
    return (w, g_kv.reshape(1, KV_LORA), g_q.reshape(1, Q_LORA),
            wk.reshape(KV_LORA, hw).astype(BF16), wv.reshape(KV_LORA, hw).astype(BF16),
            wq.reshape(Q_LORA, hw).astype(BF16), pk.astype(BF16), vone.reshape(1, hw))


def _rope_tables(n, rotate):
    ones = jnp.ones((n, NOPE), F32)
    zpad = jnp.zeros((n, HEAD_PAD - NOPE - ROPE), F32)
    if not rotate:
        c = jnp.concatenate([ones, jnp.ones((n, ROPE), F32), zpad], axis=1)
        z = jnp.zeros((n, HEAD_PAD), F32)
        return c, z, z
    nf = ROPE // 4
    rows = n // GRID_W
    lane = np.arange(HEAD_PAD)
    blk = (lane - NOPE) // (2 * nf)
    in_rope = (lane >= NOPE) & (lane < NOPE + ROPE)
    first = ((lane - NOPE) % (2 * nf)) < nf
    inv = ROPE_BASE ** (-jnp.arange(nf, dtype=F32) / nf)
    inv_lane = jnp.tile(inv, HEAD_PAD // nf)
    is_row = jnp.asarray(in_rope & (blk == 0))
    is_col = jnp.asarray(in_rope & (blk == 1))
    ang_r = jnp.arange(rows, dtype=F32)[:, None] * inv_lane[None, :]
    ang_c = jnp.arange(GRID_W, dtype=F32)[:, None] * inv_lane[None, :]

    def grid(fr, fc, base):
        t3 = (jnp.where(is_row, fr, 0.0)[:, None, :] + jnp.where(is_col, fc, 0.0)[None, :, :]
              + base[None, None, :])
        return t3.reshape(n, HEAD_PAD)

    nope_one = jnp.asarray((lane < NOPE).astype(np.float32))
    zero = jnp.zeros((HEAD_PAD,), F32)
    m1 = jnp.asarray(first)
    c = grid(jnp.cos(ang_r), jnp.cos(ang_c), nope_one)
    s1 = grid(jnp.where(m1, -jnp.sin(ang_r), 0.0), jnp.where(m1, -jnp.sin(ang_c), 0.0), zero)
    s2 = grid(jnp.where(m1, 0.0, jnp.sin(ang_r)), jnp.where(m1, 0.0, jnp.sin(ang_c)), zero)
    return c, s1, s2


def _s5_selectors():
    T, H = S5_T, S5_H
    nslot = 2 * T - 1
    toep = np.zeros((T, nslot * H, T * H), np.float32)
    for t in range(T):
        for s in range(T):
            slot = 0 if t == s else (t - s if t > s else T - 1 + s - t)
            for h in range(H):
                toep[t, slot * H + h, s * H + h] = 1.0
    e_s = np.kron(np.eye(T, dtype=np.float32), np.ones((1, H), np.float32))
    e_h = np.kron(np.ones((1, T), np.float32), np.eye(H, dtype=np.float32))
    return toep, e_s, e_h


def _prep_s5(lam_re, lam_im, log_dt, b_re, b_im, c_re, c_im, s5_d):
    T, G, P, H = S5_T, S5_G, S5_P, S5_H
    hi = lax.Precision.HIGHEST
    toep, e_s, e_h = _s5_selectors()
    lam_re = lam_re.astype(F32)
    lam_im = lam_im.astype(F32)
    dt = jnp.exp(log_dt.astype(F32))[..., None]
    zr, zi = lam_re * dt, lam_im * dt
    j = jnp.arange(T + 1, dtype=F32)[None, None, :, None]
    mag = jnp.exp(zr[:, :, None, :] * j)
    ang = zi[:, :, None, :] * j
    pr, pi = mag * jnp.cos(ang), mag * jnp.sin(ang)
    nr, ni = pr[:, :, 1] - 1.0, pi[:, :, 1]
    den = lam_re * lam_re + lam_im * lam_im
    k_re = (nr * lam_re + ni * lam_im) / den
    k_im = (ni * lam_re - nr * lam_im) / den
    bb_re = k_re[..., None] * b_re - k_im[..., None] * b_im
    bb_im = k_re[..., None] * b_im + k_im[..., None] * b_re
    cre, cim = c_re[:, :, None, :, :], c_im[:, :, None, :, :]
    prb, pib = pr[:, :, :, None, :], pi[:, :, :, None, :]
    cl_re = cre * prb - cim * pib
    cl_im = cre * pib + cim * prb
    kk = (jnp.einsum('dgjop,dgph->dgjoh', cl_re[:, :, :T], bb_re, precision=hi)
          - jnp.einsum('dgjop,dgph->dgjoh', cl_im[:, :, :T], bb_im, precision=hi))
    skip = jnp.eye(H, dtype=F32)[None] * s5_d.astype(F32).reshape(G, H, 1)
    x = jnp.concatenate([(kk[0, :, 0] + kk[1, :, 0] + skip)[:, None], kk[0, :, 1:], kk[1, :, 1:]],
                        axis=1)
    x = x.transpose(0, 2, 1, 3).reshape(G, H, (2 * T - 1) * H)
    m = jnp.einsum('goJ,tJc->gtoc', x.astype(BF16), jnp.asarray(toep, BF16),
                   preferred_element_type=F32).reshape(G, T * H, T * H)
    over_s = lambda a: jnp.einsum('gap,ac->gpc', a, jnp.asarray(e_s), precision=hi)
    over_h = lambda a: jnp.einsum('gpa,ac->gpc', a, jnp.asarray(e_h), precision=hi)

    def lb(p_r, p_i, d):
        ar, ai, br, bi = over_s(p_r), over_s(p_i), over_h(bb_re[d]), over_h(bb_im[d])
        return ar * br - ai * bi, ar * bi + ai * br

    f_re, f_im = lb(pr[0, :, T - 1::-1], pi[0, :, T - 1::-1], 0)
    b_re_, b_im_ = lb(pr[1, :, :T], pi[1, :, :T], 1)
    pb = jnp.concatenate([f_re, b_re_, f_im, b_im_], axis=1)
    to_c = lambda a: a.reshape(G, T * H, P)
    zf_re, zf_im = cl_re[0, :, 1:], cl_im[0, :, 1:]
    zb_re, zb_im = cl_re[1, :, T:0:-1], cl_im[1, :, T:0:-1]
    pc = jnp.concatenate([to_c(zf_re), to_c(zb_re), -to_c(zf_im), -to_c(zb_im)], axis=2)
    lanes = lambda a: jnp.concatenate([a[0], a[1]], axis=-1).reshape(1, G * 2 * P)
    return (m.astype(BF16), pb.astype(BF16), pc.astype(BF16), lanes(pr[:, :, T]), lanes(pi[:, :, T]))


def _prep_merge_weights(w_pool, pool_scale, w_glu, b_glu, sgu_g, sgu_b, w_s, b_s, w_out, ln_g, ln_b):
    z = jnp.zeros((POOL_GC, POOL_GC), F32)
    ng = len(POOL_WINDOWS)
    wpool = jnp.concatenate(
        [jnp.concatenate([w_pool[gi] if gj == gi else z for gj in range(ng)], axis=1)
         for gi in range(ng)], axis=0)
    bs = jnp.repeat(b_s.T, SGU_HD, axis=1)
    r = lambda a: a.reshape(1, -1).astype(F32)
    return (wpool.astype(BF16), r(pool_scale), w_glu.astype(BF16), r(b_glu), r(sgu_g), r(sgu_b),
            w_s.astype(BF16), bs.astype(F32), w_out.astype(BF16), r(ln_g), r(ln_b))


def kernel(x, c, ctx, c_ctx, w_mod, b_mod, w_in, g_q, w_uq, g_kv, w_ukv, w_pool, pool_scale, lam_re, lam_im, log_dt, s5_b_re, s5_b_im, s5_c_re, s5_c_im, s5_d, w_glu, b_glu, sgu_g, sgu_b, w_s, b_s, w_out, ln_g, ln_b):
    B, n, D = x.shape
    nc = ctx.shape[1]
    depth = w_mod.shape[0]
    alpha = (2 * depth) ** 0.25
    assert B + 1 <= 8
    tm = min(ATT_TK, n)
    tmc = min(ATT_TK, nc)
    tq = min(512, n)
    tqc = min(512, nc)

    cc = jnp.concatenate([c.astype(F32), c_ctx.astype(F32).reshape(1, D),
                          jnp.zeros((8 - B - 1, D), F32)], axis=0)
    mod = _modulation(cc, w_mod, b_mod.reshape(depth, 1, 3 * D))
    tab_l = _rope_tables(n, True)
    tab_c = _rope_tables(nc, False)

    wts_in_all = jax.vmap(_prep_inproj_weights)(w_in, g_q, w_uq, g_kv, w_ukv)
    s5_ops_all = jax.vmap(_prep_s5)(lam_re, lam_im, log_dt, s5_b_re, s5_b_im, s5_c_re, s5_c_im, s5_d)
    wts_mg_all = jax.vmap(_prep_merge_weights)(w_pool, pool_scale, w_glu, b_glu, sgu_g, sgu_b,
                                               w_s, b_s, w_out, ln_g, ln_b)

    for l in range(depth):
        last = l == depth - 1
        shift = mod[l, :, 0:D].reshape(8, 1, D)
        scale = mod[l, :, D:2 * D].reshape(8, 1, D)
        gate = mod[l, :, 2 * D:].reshape(8, 1, D)
        wts_in = tuple(a[l] for a in wts_in_all)
        s5_ops = tuple(a[l] for a in s5_ops_all)
        wts_mg = tuple(a[l] for a in wts_mg_all)

        q_l, k_l, v_l, s5u_l, mix_l = _inproj(x, shift[:B], scale[:B], True, wts_in, tab_l, tm)
        q_c, k_c, v_c, s5u_c, mix_c = _inproj(ctx, shift[B:B + 1], scale[B:B + 1], False, wts_in, tab_c, tmc)
        att_l = _attention(q_l, [k_l, k_c], [v_l, v_c], tq)
        y5_l, y5_c = _s5(s5u_l, s5u_c, s5_ops)
        x_new = _merge(x, att_l, y5_l, mix_l, gate[:B], True, wts_mg, tm, alpha)
        if not last:
            att_c = _attention(q_c, [k_c], [v_c], tqc)
            ctx = _merge(ctx, att_c, y5_c, mix_c, gate[B:B + 1], False, wts_mg, tmc, alpha)
        x = x_new
    return x
```

```python
import functools
import math

import jax
import jax.numpy as jnp
import numpy as np
from jax import lax
from jax.experimental import pallas as pl
from jax.experimental.pallas import tpu as pltpu

F32 = jnp.float32
BF16 = jnp.bfloat16

GRID_W = 64
N_HEADS = 4
NOPE = 64
ROPE = 32
V_DIM = 64
Q_LORA = 192
KV_LORA = 128
ROPE_BASE = 10000.0
BRANCH = 256
POOL_WINDOWS = (2, 4, 8, 16)
POOL_GC = 64
S5_H = 16
S5_G = 16
S5_P = 64
SGU_HEADS = 4
SGU_HD = 64
CHUNK = 128
LN_EPS = 1e-6

LANE = 128
HEAD_PAD = 128
V_ROWS = 80
V7X_VMEM_LIMIT = 56 * 1024 * 1024
S5_T = 16
S5_BLK = S5_T * LANE

C_CKV = 0
C_CQ = 128
C_KR = 384
C_S5 = 512
C_MIX = 768
IN_PAD = 2560
MIX_W = 3 * BRANCH + 4 * BRANCH


def _dot(a, b):
    return jnp.dot(a, b, preferred_element_type=F32)


def _dot_nt(a, b):
    return lax.dot_general(a, b, (((1,), (1,)), ((), ())), preferred_element_type=F32)


def _ln_rows(x):
    mu = jnp.mean(x, axis=-1, keepdims=True)
    xc = x - mu
    var = jnp.mean(xc * xc, axis=-1, keepdims=True)
    return xc * lax.rsqrt(var + LN_EPS)


def _sigmoid(x):
    return 0.5 * jnp.tanh(0.5 * x) + 0.5


def _mod_kernel(c_ref, w_ref, b_ref, o_ref):
    c = c_ref[...]
    s = c * _sigmoid(c)
    o_ref[0] = jnp.dot(s, w_ref[0], preferred_element_type=F32,
                       precision=lax.Precision.HIGHEST) + b_ref[0]


def _modulation(cc, w_mod, b_mod):
    L, D, D3 = w_mod.shape
    nb = D3 // D
    return pl.pallas_call(
        _mod_kernel,
        out_shape=jax.ShapeDtypeStruct((L, 8, D3), F32),
        grid=(L, nb),
        in_specs=[pl.BlockSpec((8, D), lambda l, j: (0, 0)),
                  pl.BlockSpec((1, D, D), lambda l, j: (l, 0, j)),
                  pl.BlockSpec((1, 1, D), lambda l, j: (l, 0, j))],
        out_specs=pl.BlockSpec((1, 8, D), lambda l, j: (l, 0, j)),
        compiler_params=pltpu.CompilerParams(vmem_limit_bytes=V7X_VMEM_LIMIT),
        name="modulation",
    )(cc, w_mod, b_mod)


def _rope_lanes(x, c, s1, s2):
    w = x.shape[-1]
    return x * c + pltpu.roll(x, w - 8, 1) * s1 + pltpu.roll(x, 8, 1) * s2


def _inproj_kernel(x_ref, shift_ref, scale_ref, w_ref, gkv_ref, gq_ref, wk_ref, wv_ref, wq_ref,
                   pk_ref, vone_ref, rc_ref, rs1_ref, rs2_ref,
                   q_ref, k_ref, vt_ref, s5_ref, mix_ref):
    x = x_ref[0]
    h = _ln_rows(x) * (1.0 + scale_ref[0]) + shift_ref[0]
    proj = _dot(h.astype(BF16), w_ref[...])
    ckv = proj[:, C_CKV:C_CKV + KV_LORA]
    cq = proj[:, C_CQ:C_CQ + Q_LORA]
    kr = proj[:, C_KR:C_KR + LANE]

    rk = ckv * lax.rsqrt(jnp.mean(ckv * ckv, axis=-1, keepdims=True) + LN_EPS) * gkv_ref[...]
    rk = rk.astype(BF16)
    rq = cq * lax.rsqrt(jnp.mean(cq * cq, axis=-1, keepdims=True) + LN_EPS) * gq_ref[...]
    rq = rq.astype(BF16)

    rc, rs1, rs2 = rc_ref[...], rs1_ref[...], rs2_ref[...]
    krp = _rope_lanes(_dot(kr.astype(BF16), pk_ref[...]), rc, rs1, rs2)
    kn = _dot(rk, wk_ref[...])
    vv = _dot(rk, wv_ref[...]) + vone_ref[...]
    qq = _dot(rq, wq_ref[...])
    for hd in range(N_HEADS):
        sl = slice(hd * HEAD_PAD, (hd + 1) * HEAD_PAD)
        q_ref[0, hd] = _rope_lanes(qq[:, sl], rc, rs1, rs2).astype(BF16)
        k_ref[0, hd] = (kn[:, sl] + krp).astype(BF16)
        vt_ref[0, hd, 0] = vv[:, sl].T[:V_ROWS].astype(BF16)
    s5_ref[0] = proj[:, C_S5:C_S5 + BRANCH].astype(BF16)
    mix_ref[0] = proj[:, C_MIX:C_MIX + MIX_W].astype(BF16)


def _inproj(x, shift, scale, batch_mod, wts, tables, tm):
    B, n, D = x.shape
    w_in, gkv, gq, wk, wv, wq, pk, vone = wts
    rc, rs1, rs2 = tables
    mod_map = (lambda b, i: (b, 0, 0)) if batch_mod else (lambda b, i: (0, 0, 0))
    full2 = lambda b, i: (0, 0)
    tab = pl.BlockSpec((tm, LANE), lambda b, i: (i, 0))
    hw = N_HEADS * HEAD_PAD
    return pl.pallas_call(
        _inproj_kernel,
        out_shape=(jax.ShapeDtypeStruct((B, N_HEADS, n, HEAD_PAD), BF16),
                   jax.ShapeDtypeStruct((B, N_HEADS, n, HEAD_PAD), BF16),
                   jax.ShapeDtypeStruct((B, N_HEADS, n // tm, V_ROWS, tm), BF16),
                   jax.ShapeDtypeStruct((B, n, BRANCH), BF16),
                   jax.ShapeDtypeStruct((B, n, MIX_W), BF16)),
        grid=(B, n // tm),
        in_specs=[pl.BlockSpec((1, tm, D), lambda b, i: (b, i, 0)),
                  pl.BlockSpec((1, 1, D), mod_map),
                  pl.BlockSpec((1, 1, D), mod_map),
                  pl.BlockSpec((D, IN_PAD), full2),
                  pl.BlockSpec((1, KV_LORA), full2),
                  pl.BlockSpec((1, Q_LORA), full2),
                  pl.BlockSpec((KV_LORA, hw), full2),
                  pl.BlockSpec((KV_LORA, hw), full2),
                  pl.BlockSpec((Q_LORA, hw), full2),
                  pl.BlockSpec((LANE, LANE), full2),
                  pl.BlockSpec((1, hw), full2),
                  tab, tab, tab],
        out_specs=(pl.BlockSpec((1, N_HEADS, tm, HEAD_PAD), lambda b, i: (b, 0, i, 0)),
                   pl.BlockSpec((1, N_HEADS, tm, HEAD_PAD), lambda b, i: (b, 0, i, 0)),
                   pl.BlockSpec((1, N_HEADS, 1, V_ROWS, tm), lambda b, i: (b, 0, i, 0, 0)),
                   pl.BlockSpec((1, tm, BRANCH), lambda b, i: (b, i, 0)),
                   pl.BlockSpec((1, tm, MIX_W), lambda b, i: (b, i, 0))),
        compiler_params=pltpu.CompilerParams(
            dimension_semantics=("parallel", "parallel"), vmem_limit_bytes=V7X_VMEM_LIMIT),
        name="inproj",
    )(x, shift, scale, w_in, gkv, gq, wk, wv, wq, pk, vone, rc, rs1, rs2)


ATT_TK = 512


def _attn_kernel(*refs, n_src):
    q_ref = refs[0]
    k_refs = refs[1:1 + n_src]
    vt_refs = refs[1 + n_src:1 + 2 * n_src]
    o_ref = refs[1 + 2 * n_src]
    s_scr, pa_scr, pb_scr, acc_scr, m_scr = refs[2 + 2 * n_src:]
    tq = q_ref.shape[2]
    step = pl.program_id(1)

    @pl.when(step == 0)
    def _():
        s_scr[...] = jnp.zeros(s_scr.shape, F32)
        m_scr[...] = jnp.zeros(m_scr.shape, F32)

    q = q_ref[0, 0]
    m_prev = jnp.max(m_scr[...], axis=0, keepdims=True)
    acc_scr[...] = jnp.zeros(acc_scr.shape, F32)
    m_acc = jnp.full((8, tq), -jnp.inf, F32)

    tiles = []
    row = 0
    for si in range(n_src):
        nt, tk = vt_refs[si].shape[2], vt_refs[si].shape[4]
        for j in range(nt):
            tiles.append((si, j, row, tk))
            row += tk
    p_bufs = ((pa_scr, 0), (pa_scr, 1), (pb_scr, 0), (pb_scr, 1))

    def ex(i):
        si, j, r, tk = tiles[i]
        buf, slot = p_bufs[i % len(p_bufs)]
        buf[slot, 0:tk, :] = jnp.exp2((s_scr[r:r + tk, :] - m_prev).astype(BF16))

    def qk(i, m):
        si, j, r, tk = tiles[i]
        s = _dot_nt(k_refs[si][0, 0, j * tk:(j + 1) * tk, :], q)
        s_scr[r:r + tk, :] = s
        parts = [s[a:a + 8, :] for a in range(0, tk, 8)]
        while len(parts) > 1:
            parts = [jnp.maximum(parts[a], parts[a + 1]) for a in range(0, len(parts), 2)]
        return jnp.maximum(m, parts[0])

    def mm(i):
        si, j, r, tk = tiles[i]
        buf, slot = p_bufs[i % len(p_bufs)]
        acc_scr[...] += _dot(vt_refs[si][0, 0, j], buf[slot, 0:tk, :])

    ex(0)
    for i in range(len(tiles)):
        if i + 1 < len(tiles):
            ex(i + 1)
        m_acc = qk(i, m_acc)
        mm(i)

    m_scr[...] = m_acc

    @pl.when(step > 0)
    def _():
        acc = acc_scr[...]
        o_ref[0, 0] = (acc[:V_DIM, :] / acc[V_DIM:V_DIM + 1, :]).astype(o_ref.dtype)


def _attention(q, ks, vts, tq):
    B, H, n, _ = q.shape
    n_src = len(ks)
    nkeys = sum(k.shape[2] for k in ks)
    tk_max = max(v.shape[4] for v in vts)
    for k, v in zip(ks, vts):
        nt = v.shape[2]
        assert k.shape[2] == nt * v.shape[4]
    nq = n // tq
    n_items = H * nq
    cur = lambda k: jnp.minimum(k, n_items - 1)
    prv = lambda k: jnp.maximum(k - 1, 0)
    k_spec = lambda a: pl.BlockSpec((1, 1) + a.shape[2:], lambda b, k: (b, cur(k) // nq, 0, 0))
    vt_spec = lambda a: pl.BlockSpec((1, 1) + a.shape[2:], lambda b, k: (b, prv(k) // nq, 0, 0, 0))
    return pl.pallas_call(
        functools.partial(_attn_kernel, n_src=n_src),
        out_shape=jax.ShapeDtypeStruct((B, H, V_DIM, n), BF16),
        grid=(B, n_items + 1),
        in_specs=[pl.BlockSpec((1, 1, tq, HEAD_PAD), lambda b, k: (b, cur(k) // nq, cur(k) % nq, 0))]
                 + [k_spec(k) for k in ks] + [vt_spec(v) for v in vts],
        out_specs=pl.BlockSpec((1, 1, V_DIM, tq), lambda b, k: (b, prv(k) // nq, 0, prv(k) % nq)),
        scratch_shapes=[pltpu.VMEM((nkeys, tq), F32),
                        pltpu.VMEM((2, tk_max, tq), BF16), pltpu.VMEM((2, tk_max, tq), BF16),
                        pltpu.VMEM((V_ROWS, tq), F32), pltpu.VMEM((8, tq), F32)],
        compiler_params=pltpu.CompilerParams(
            dimension_semantics=("parallel", "arbitrary"), vmem_limit_bytes=V7X_VMEM_LIMIT),
        name="attention",
    )(q, *ks, *vts)


def _s5_kernel(ul_ref, uc_ref, m_ref, pb_ref, pc_ref, are_ref, aim_ref, yl_ref, yc_ref,
               ut_ref, sre_ref, sim_ref, f_ref):
    n = ul_ref.shape[1]
    nc = uc_ref.shape[1]
    kl = n // S5_T
    kc = nc // S5_T
    nblk = n // S5_BLK
    rows_blk = S5_BLK // S5_T
    ncol = ut_ref.shape[2]

    half_g = S5_G // 2

    def relayout_in(nrows, col0):
        for s in range(S5_T):
            for h in range(2):
                blk = f_ref[h, pl.ds(s, nrows, stride=S5_T), :]
                if nrows < LANE:
                    blk = jnp.concatenate([blk, jnp.zeros((LANE - nrows, LANE), F32)], axis=0)
                bt = blk.T
                for gg in range(half_g):
                    ut_ref[h * half_g + gg, s * S5_H:(s + 1) * S5_H, col0:col0 + LANE] = (
                        bt[gg * S5_H:(gg + 1) * S5_H, :].astype(BF16))

    for jb in range(nblk):
        for h in range(2):
            f_ref[h] = ul_ref[0, jb * S5_BLK:(jb + 1) * S5_BLK, h * LANE:(h + 1) * LANE].astype(F32)
        relayout_in(rows_blk, jb * LANE)
    for h in range(2):
        f_ref[h, 0:nc, :] = uc_ref[0, :, h * LANE:(h + 1) * LANE].astype(F32)
    relayout_in(kc, kl)

    for g in range(S5_G):
        st = _dot(pb_ref[g], ut_ref[g])
        stt = st.T
        sre_ref[:, g * LANE:(g + 1) * LANE] = stt[:, :LANE]
        sim_ref[:, g * LANE:(g + 1) * LANE] = stt[:, LANE:]

    are = are_ref[...]
    aim = aim_ref[...]
    lane = lax.broadcasted_iota(jnp.int32, (1, S5_G * LANE), 1)
    is_f = (lane % LANE) < S5_P

    def step(rf, rb, carry):
        hr, hi = carry
        srf, sif = sre_ref[pl.ds(rf, 1), :], sim_ref[pl.ds(rf, 1), :]
        srb, sib = sre_ref[pl.ds(rb, 1), :], sim_ref[pl.ds(rb, 1), :]
        sre_ref[pl.ds(rf, 1), :] = jnp.where(is_f, hr, srf)
        sim_ref[pl.ds(rf, 1), :] = jnp.where(is_f, hi, sif)
        sre_ref[pl.ds(rb, 1), :] = jnp.where(is_f, srb, hr)
        sim_ref[pl.ds(rb, 1), :] = jnp.where(is_f, sib, hi)
        sr = jnp.where(is_f, srf, srb)
        si = jnp.where(is_f, sif, sib)
        return (are * hr - aim * hi + sr, are * hi + aim * hr + si)

    zero = jnp.zeros((1, S5_G * LANE), F32)
    carry = lax.fori_loop(0, kc, lambda i, c: step(kl + i, kl + kc - 1 - i, c), (zero, zero))
    lax.fori_loop(0, kl, lambda i, c: step(i, kl - 1 - i, c), carry)

    for g in range(S5_G):
        hin = jnp.concatenate([sre_ref[:, g * LANE:(g + 1) * LANE],
                               sim_ref[:, g * LANE:(g + 1) * LANE]], axis=1)
        hin_t = hin.T.astype(BF16)
        y = _dot(m_ref[g], ut_ref[g]) + _dot(pc_ref[g], hin_t)
        ut_ref[g] = y.astype(BF16)

    def relayout_out(nrows, col0):
        for t in range(S5_T):
            for h in range(2):
                yt = jnp.concatenate(
                    [ut_ref[h * half_g + gg, t * S5_H:(t + 1) * S5_H, col0:col0 + LANE]
                     for gg in range(half_g)], axis=0).astype(F32)
                f_ref[h, pl.ds(t, nrows, stride=S5_T), :] = yt.T[:nrows]

    for jb in range(nblk):
        relayout_out(rows_blk, jb * LANE)
        for h in range(2):
            yl_ref[0, jb * S5_BLK:(jb + 1) * S5_BLK, h * LANE:(h + 1) * LANE] = (
                f_ref[h].astype(yl_ref.dtype))
    relayout_out(kc, kl)
    for h in range(2):
        yc_ref[0, :, h * LANE:(h + 1) * LANE] = f_ref[h, 0:nc, :].astype(yc_ref.dtype)


def _s5(ul, uc, ops):
    B, n, _ = ul.shape
    nc = uc.shape[1]
    m, pb, pc, are, aim = ops
    assert n % S5_BLK == 0 and nc % S5_T == 0 and nc // S5_T <= LANE and nc <= S5_BLK
    assert (n // S5_T) % 2 == 0 and (nc // S5_T) % 2 == 0
    ncol = n // S5_T + LANE
    w3 = lambda b: (0, 0, 0)
    return pl.pallas_call(
        _s5_kernel,
        out_shape=(jax.ShapeDtypeStruct((B, n, BRANCH), BF16),
                   jax.ShapeDtypeStruct((B, nc, BRANCH), BF16)),
        grid=(B,),
        in_specs=[pl.BlockSpec((1, n, BRANCH), lambda b: (b, 0, 0)),
                  pl.BlockSpec((1, nc, BRANCH), lambda b: (b, 0, 0)),
                  pl.BlockSpec(m.shape, w3), pl.BlockSpec(pb.shape, w3), pl.BlockSpec(pc.shape, w3),
                  pl.BlockSpec(are.shape, lambda b: (0, 0)), pl.BlockSpec(aim.shape, lambda b: (0, 0))],
        out_specs=(pl.BlockSpec((1, n, BRANCH), lambda b: (b, 0, 0)),
                   pl.BlockSpec((1, nc, BRANCH), lambda b: (b, 0, 0))),
        scratch_shapes=[pltpu.VMEM((S5_G, S5_T * S5_H, ncol), BF16),
                        pltpu.VMEM((ncol, S5_G * LANE), F32),
                        pltpu.VMEM((ncol, S5_G * LANE), F32),
                        pltpu.VMEM((2, S5_BLK, LANE), F32)],
        compiler_params=pltpu.CompilerParams(
            dimension_semantics=("parallel",), vmem_limit_bytes=V7X_VMEM_LIMIT),
        name="s5",
    )(ul, uc, m, pb, pc, are, aim)


POOL_HALO = 16


def _gelu_tanh(x):
    return 0.5 * x * (1.0 + jnp.tanh(math.sqrt(2.0 / math.pi) * (x + 0.044715 * (x * x * x))))


def _merge_kernel(x_ref, att_ref, y5_ref, mix_ref, prev_ref, next_ref, gate_ref,
                  wpool_ref, pscale_ref, wglu_ref, bglu_ref, sg_ref, sb_ref, ws_ref, bs_ref,
                  wout_ref, lng_ref, lnb_ref, o_ref, a_ref, b2_ref, b4_ref, b8_ref,
                  *, n_total, alpha):
    tm = x_ref.shape[1]
    i = pl.program_id(1)
    ni = pl.num_programs(1)
    hl = POOL_HALO

    pool_in = mix_ref[0, :, 0:BRANCH].astype(F32)
    a_ref[0:hl, :] = jnp.where(i > 0, prev_ref[0].astype(F32), 0.0)
    a_ref[hl:hl + tm, :] = pool_in
    a_ref[hl + tm:hl + tm + hl, :] = jnp.where(i < ni - 1, next_ref[0].astype(F32), 0.0)
    r2 = tm + 16
    b2_ref[8:8 + r2, :] = a_ref[7:7 + r2, :] + a_ref[8:8 + r2, :]
    r4 = tm + 12
    b4_ref[10:10 + r4, :] = b2_ref[9:9 + r4, :] + b2_ref[11:11 + r4, :]
    r8 = tm + 8
    b8_ref[12:12 + r8, :] = b4_ref[10:10 + r8, :] + b4_ref[14:14 + r8, :]
    c2 = b2_ref[hl:hl + tm, :]
    c4 = b4_ref[hl:hl + tm, :]
    c8 = b8_ref[hl:hl + tm, :]
    c16 = b8_ref[hl - 4:hl - 4 + tm, :] + b8_ref[hl + 4:hl + 4 + tm, :]
    lane = lax.broadcasted_iota(jnp.int32, (tm, BRANCH), 1)
    grp = lane // POOL_GC
    tot = jnp.where(grp == 0, c2, jnp.where(grp == 1, c4, jnp.where(grp == 2, c8, c16)))
    t = lax.broadcasted_iota(jnp.int32, (tm, BRANCH), 0) + i * tm
    half = jnp.left_shift(1, grp)
    cnt = jnp.minimum(t + half, n_total) - jnp.maximum(t - half, 0)
    pooled = tot / cnt.astype(F32) - pool_in
    pool_l = _dot(pooled.astype(BF16), wpool_ref[...]) * pscale_ref[...]

    g1 = _gelu_tanh(y5_ref[0].astype(F32))
    s5_l = g1 * _sigmoid(_dot(g1.astype(BF16), wglu_ref[...]) + bglu_ref[...])

    su = mix_ref[0, :, BRANCH:2 * BRANCH].astype(F32)
    sv = mix_ref[0, :, 2 * BRANCH:3 * BRANCH].astype(F32)
    vn = (_ln_rows(sv) * sg_ref[...] + sb_ref[...]).astype(BF16)
    lane_c = lax.broadcasted_iota(jnp.int32, (CHUNK, BRANCH), 1) // SGU_HD
    mixed = []
    for c in range(tm // CHUNK):
        vc = vn[c * CHUNK:(c + 1) * CHUNK, :]
        r = _dot(ws_ref[SGU_HEADS - 1], vc)
        for hd in range(SGU_HEADS - 2, -1, -1):
            r = jnp.where(lane_c == hd, _dot(ws_ref[hd], vc), r)
        mixed.append(r + bs_ref[...])
    sgu_l = su * jnp.concatenate(mixed, axis=0)

    gts = mix_ref[0, :, 3 * BRANCH:].astype(F32)
    att = att_ref[0].astype(F32).reshape(N_HEADS * V_DIM, tm).T
    cat = jnp.concatenate([att, pool_l, s5_l, sgu_l], axis=-1)
    y = _dot((cat * (gts * _sigmoid(gts))).astype(BF16), wout_ref[...])
    z = alpha * x_ref[0] + gate_ref[0] * y
    o_ref[0] = _ln_rows(z) * lng_ref[...] + lnb_ref[...]


def _merge(x, att, y5, mix, gate, batch_mod, wts, tm, alpha):
    B, n, D = x.shape
    wpool, pscale, wglu, bglu, sg, sb, ws, bs, wout, lng, lnb = wts
    hb = tm // POOL_HALO
    nhb = n // POOL_HALO
    full2 = lambda b, i: (0, 0)
    mod_map = (lambda b, i: (b, 0, 0)) if batch_mod else (lambda b, i: (0, 0, 0))
    row = lambda w: pl.BlockSpec((1, w), full2)
    return pl.pallas_call(
        functools.partial(_merge_kernel, n_total=n, alpha=alpha),
        out_shape=jax.ShapeDtypeStruct((B, n, D), F32),
        grid=(B, n // tm),
        in_specs=[pl.BlockSpec((1, tm, D), lambda b, i: (b, i, 0)),
                  pl.BlockSpec((1, N_HEADS, V_DIM, tm), lambda b, i: (b, 0, 0, i)),
                  pl.BlockSpec((1, tm, BRANCH), lambda b, i: (b, i, 0)),
                  pl.BlockSpec((1, tm, MIX_W), lambda b, i: (b, i, 0)),
                  pl.BlockSpec((1, POOL_HALO, BRANCH),
                               lambda b, i: (b, jnp.maximum(i * hb - 1, 0), 0)),
                  pl.BlockSpec((1, POOL_HALO, BRANCH),
                               lambda b, i: (b, jnp.minimum((i + 1) * hb, nhb - 1), 0)),
                  pl.BlockSpec((1, 1, D), mod_map),
                  pl.BlockSpec((BRANCH, BRANCH), full2), row(BRANCH),
                  pl.BlockSpec((BRANCH, BRANCH), full2), row(BRANCH),
                  row(BRANCH), row(BRANCH),
                  pl.BlockSpec((SGU_HEADS, CHUNK, CHUNK), lambda b, i: (0, 0, 0)),
                  pl.BlockSpec((CHUNK, BRANCH), full2),
                  pl.BlockSpec((4 * BRANCH, D), full2), row(D), row(D)],
        out_specs=pl.BlockSpec((1, tm, D), lambda b, i: (b, i, 0)),
        scratch_shapes=[pltpu.VMEM((tm + 2 * POOL_HALO, BRANCH), F32) for _ in range(4)],
        compiler_params=pltpu.CompilerParams(
            dimension_semantics=("parallel", "parallel"), vmem_limit_bytes=V7X_VMEM_LIMIT),
        name="merge",
    )(x, att, y5, mix, mix, mix, gate, wpool, pscale, wglu, bglu, sg, sb, ws, bs, wout, lng, lnb)


def _prep_inproj_weights(w_in, g_q, w_uq, g_kv, w_ukv):
    D = w_in.shape[0]
    o_kr = KV_LORA
    o_s5 = o_kr + ROPE
    o_cq = o_s5 + BRANCH
    o_rest = o_cq + Q_LORA
    wb = w_in.astype(BF16)
    zpad = lambda k: jnp.zeros((D, k), BF16)
    w = jnp.concatenate(
        [wb[:, 0:KV_LORA], wb[:, o_cq:o_cq + Q_LORA], zpad(C_KR - C_CQ - Q_LORA),
         wb[:, o_kr:o_kr + ROPE], zpad(C_S5 - C_KR - ROPE), wb[:, o_s5:o_s5 + BRANCH],
         wb[:, o_rest:]], axis=1)
    assert w.shape == (D, IN_PAD)
    hw = N_HEADS * HEAD_PAD
    scale = (NOPE + ROPE) ** -0.5 * math.log2(math.e)
    ukv = w_ukv.reshape(KV_LORA, N_HEADS, NOPE + V_DIM)
    uq = w_uq.reshape(Q_LORA, N_HEADS, NOPE + ROPE) * scale
    padl = lambda a: jnp.pad(a, ((0, 0), (0, 0), (0, HEAD_PAD - a.shape[2])))
    wk = padl(ukv[:, :, :NOPE])
    wv = padl(ukv[:, :, NOPE:])
    wq = padl(uq)
    pk = (jnp.arange(LANE)[:, None] + NOPE == jnp.arange(LANE)[None, :]) & (jnp.arange(LANE)[:, None] < ROPE)
    vone = jnp.tile(jnp.arange(HEAD_PAD) == V_DIM, N_HEADS).astype(F32)
    return (w, g_kv.reshape(1, KV_LORA), g_q.reshape(1, Q_LORA),
            wk.reshape(KV_LORA, hw).astype(BF16), wv.reshape(KV_LORA, hw).astype(BF16),
            wq.reshape(Q_LORA, hw).astype(BF16), pk.astype(BF16), vone.reshape(1, hw))


def _rope_tables(n, rotate):
    ones = jnp.ones((n, NOPE), F32)
    zpad = jnp.zeros((n, HEAD_PAD - NOPE - ROPE), F32)
    if not rotate:
        c = jnp.concatenate([ones, jnp.ones((n, ROPE), F32), zpad], axis=1)
        z = jnp.zeros((n, HEAD_PAD), F32)
        return c, z, z
    nf = ROPE // 4
    rows = n // GRID_W
    lane = np.arange(HEAD_PAD)
    blk = (lane - NOPE) // (2 * nf)
    in_rope = (lane >= NOPE) & (lane < NOPE + ROPE)
    first = ((lane - NOPE) % (2 * nf)) < nf
    inv = ROPE_BASE ** (-jnp.arange(nf, dtype=F32) / nf)
    inv_lane = jnp.tile(inv, HEAD_PAD // nf)
    is_row = jnp.asarray(in_rope & (blk == 0))
    is_col = jnp.asarray(in_rope & (blk == 1))
    ang_r = jnp.arange(rows, dtype=F32)[:, None] * inv_lane[None, :]
    ang_c = jnp.arange(GRID_W, dtype=F32)[:, None] * inv_lane[None, :]

    def grid(fr, fc, base):
        t3 = (jnp.where(is_row, fr, 0.0)[:, None, :] + jnp.where(is_col, fc, 0.0)[None, :, :]
              + base[None, None, :])
        return t3.reshape(n, HEAD_PAD)

    nope_one = jnp.asarray((lane < NOPE).astype(np.float32))
    zero = jnp.zeros((HEAD_PAD,), F32)
    m1 = jnp.asarray(first)
    c = grid(jnp.cos(ang_r), jnp.cos(ang_c), nope_one)
    s1 = grid(jnp.where(m1, -jnp.sin(ang_r), 0.0), jnp.where(m1, -jnp.sin(ang_c), 0.0), zero)
    s2 = grid(jnp.where(m1, 0.0, jnp.sin(ang_r)), jnp.where(m1, 0.0, jnp.sin(ang_c)), zero)
    return c, s1, s2


def _s5_selectors():
    T, H = S5_T, S5_H
    nslot = 2 * T - 1
    toep = np.zeros((T, nslot * H, T * H), np.float32)
    for t in range(T):
        for s in range(T):
            slot = 0 if t == s else (t - s if t > s else T - 1 + s - t)
            for h in range(H):
                toep[t, slot * H + h, s * H + h] = 1.0
    e_s = np.kron(np.eye(T, dtype=np.float32), np.ones((1, H), np.float32))
    e_h = np.kron(np.ones((1, T), np.float32), np.eye(H, dtype=np.float32))
    return toep, e_s, e_h


def _prep_s5(lam_re, lam_im, log_dt, b_re, b_im, c_re, c_im, s5_d):
    T, G, P, H = S5_T, S5_G, S5_P, S5_H
    hi = lax.Precision.HIGHEST
    toep, e_s, e_h = _s5_selectors()
    lam_re = lam_re.astype(F32)
    lam_im = lam_im.astype(F32)
    dt = jnp.exp(log_dt.astype(F32))[..., None]
    zr, zi = lam_re * dt, lam_im * dt
    j = jnp.arange(T + 1, dtype=F32)[None, None, :, None]
    mag = jnp.exp(zr[:, :, None, :] * j)
    ang = zi[:, :, None, :] * j
    pr, pi = mag * jnp.cos(ang), mag * jnp.sin(ang)
    nr, ni = pr[:, :, 1] - 1.0, pi[:, :, 1]
    den = lam_re * lam_re + lam_im * lam_im
    k_re = (nr * lam_re + ni * lam_im) / den
    k_im = (ni * lam_re - nr * lam_im) / den
    bb_re = k_re[..., None] * b_re - k_im[..., None] * b_im
    bb_im = k_re[..., None] * b_im + k_im[..., None] * b_re
    cre, cim = c_re[:, :, None, :, :], c_im[:, :, None, :, :]
    prb, pib = pr[:, :, :, None, :], pi[:, :, :, None, :]
    cl_re = cre * prb - cim * pib
    cl_im = cre * pib + cim * prb
    kk = (jnp.einsum('dgjop,dgph->dgjoh', cl_re[:, :, :T], bb_re, precision=hi)
          - jnp.einsum('dgjop,dgph->dgjoh', cl_im[:, :, :T], bb_im, precision=hi))
    skip = jnp.eye(H, dtype=F32)[None] * s5_d.astype(F32).reshape(G, H, 1)
    x = jnp.concatenate([(kk[0, :, 0] + kk[1, :, 0] + skip)[:, None], kk[0, :, 1:], kk[1, :, 1:]],
                        axis=1)
    x = x.transpose(0, 2, 1, 3).reshape(G, H, (2 * T - 1) * H)
    m = jnp.einsum('goJ,tJc->gtoc', x.astype(BF16), jnp.asarray(toep, BF16),
                   preferred_element_type=F32).reshape(G, T * H, T * H)
    over_s = lambda a: jnp.einsum('gap,ac->gpc', a, jnp.asarray(e_s), precision=hi)
    over_h = lambda a: jnp.einsum('gpa,ac->gpc', a, jnp.asarray(e_h), precision=hi)

    def lb(p_r, p_i, d):
        ar, ai, br, bi = over_s(p_r), over_s(p_i), over_h(bb_re[d]), over_h(bb_im[d])
        return ar * br - ai * bi, ar * bi + ai * br

    f_re, f_im = lb(pr[0, :, T - 1::-1], pi[0, :, T - 1::-1], 0)
    b_re_, b_im_ = lb(pr[1, :, :T], pi[1, :, :T], 1)
    pb = jnp.concatenate([f_re, b_re_, f_im, b_im_], axis=1)
    to_c = lambda a: a.reshape(G, T * H, P)
    zf_re, zf_im = cl_re[0, :, 1:], cl_im[0, :, 1:]
    zb_re, zb_im = cl_re[1, :, T:0:-1], cl_im[1, :, T:0:-1]
    pc = jnp.concatenate([to_c(zf_re), to_c(zb_re), -to_c(zf_im), -to_c(zb_im)], axis=2)
    lanes = lambda a: jnp.concatenate([a[0], a[1]], axis=-1).reshape(1, G * 2 * P)
    return (m.astype(BF16), pb.astype(BF16), pc.astype(BF16), lanes(pr[:, :, T]), lanes(pi[:, :, T]))


def _prep_merge_weights(w_pool, pool_scale, w_glu, b_glu, sgu_g, sgu_b, w_s, b_s, w_out, ln_g, ln_b):
    z = jnp.zeros((POOL_GC, POOL_GC), F32)
    ng = len(POOL_WINDOWS)
    wpool = jnp.concatenate(
        [jnp.concatenate([w_pool[gi] if gj == gi else z for gj in range(ng)], axis=1)
         for gi in range(ng)], axis=0)
    bs = jnp.repeat(b_s.T, SGU_HD, axis=1)
    r = lambda a: a.reshape(1, -1).astype(F32)
    return (wpool.astype(BF16), r(pool_scale), w_glu.astype(BF16), r(b_glu), r(sgu_g), r(sgu_b),
            w_s.astype(BF16), bs.astype(F32), w_out.astype(BF16), r(ln_g), r(ln_b))


def kernel(x, c, ctx, c_ctx, w_mod, b_mod, w_in, g_q, w_uq, g_kv, w_ukv, w_pool, pool_scale, lam_re, lam_im, log_dt, s5_b_re, s5_b_im, s5_c_re, s5_c_im, s5_d, w_glu, b_glu, sgu_g, sgu_b, w_s, b_s, w_out, ln_g, ln_b):
    B, n, D = x.shape
    nc = ctx.shape[1]
    depth = w_mod.shape[0]
    alpha = (2 * depth) ** 0.25
    assert B + 1 <= 8
    tm = min(ATT_TK, n)
    tmc = min(ATT_TK, nc)
    tq = min(512, n)
    tqc = min(512, nc)

    cc = jnp.concatenate([c.astype(F32), c_ctx.astype(F32).reshape(1, D),
                          jnp.zeros((8 - B - 1, D), F32)], axis=0)
    mod = _modulation(cc, w_mod, b_mod.reshape(depth, 1, 3 * D))
    tab_l = _rope_tables(n, True)
    tab_c = _rope_tables(nc, False)

    wts_in_all = jax.vmap(_prep_inproj_weights)(w_in, g_q, w_uq, g_kv, w_ukv)
    s5_ops_all = jax.vmap(_prep_s5)(lam_re, lam_im, log_dt, s5_b_re, s5_b_im, s5_c_re, s5_c_im, s5_d)
    wts_mg_all = jax.vmap(_prep_merge_weights)(w_pool, pool_scale, w_glu, b_glu, sgu_g, sgu_b,
                                               w_s, b_s, w_out, ln_g, ln_b)

    for l in range(depth):
        last = l == depth - 1
        shift = mod[l, :, 0:D].reshape(8, 1, D)
        scale = mod[l, :, D:2 * D].reshape(8, 1, D)
        gate = mod[l, :, 2 * D:].reshape(8, 1, D)
        wts_in = tuple(a[l] for a in wts_in_all)
        s5_ops = tuple(a[l] for a in s5_ops_all)
        wts_mg = tuple(a[l] for a in wts_mg_all)

        q_l, k_l, v_l, s5u_l, mix_l = _inproj(x, shift[:B], scale[:B], True, wts_in, tab_l, tm)
        q_c, k_c, v_c, s5u_c, mix_c = _inproj(ctx, shift[B:B + 1], scale[B:B + 1], False, wts_in, tab_c, tmc)
        att_l = _attention(q_l, [k_l, k_c], [v_l, v_c], tq)
        y5_l, y5_c = _s5(s5u_l, s5u_c, s5_ops)
        x_new = _merge(x, att_l, y5_l, mix_l, gate[:B], True, wts_mg, tm, alpha)
        if not last:
            att_c = _attention(q_c, [k_c], [v_c], tqc)
            ctx = _merge(ctx, att_c, y5_c, mix_c, gate[B:B + 1], False, wts_mg, tmc, alpha)
        x = x_new
    return x
```

```python
import functools
import math

import jax
import jax.numpy as jnp
import numpy as np
from jax import lax
from jax.experimental import pallas as pl
from jax.experimental.pallas import tpu as pltpu

F32 = jnp.float32
BF16 = jnp.bfloat16

GRID_W = 64
N_HEADS = 4
NOPE = 64
ROPE = 32
V_DIM = 64
Q_LORA = 192
KV_LORA = 128
ROPE_BASE = 10000.0
BRANCH = 256
POOL_WINDOWS = (2, 4, 8, 16)
POOL_GC = 64
S5_H = 16
S5_G = 16
S5_P = 64
SGU_HEADS = 4
SGU_HD = 64
CHUNK = 128
LN_EPS = 1e-6

LANE = 128
HEAD_PAD = 128
V_ROWS = 80
V7X_VMEM_LIMIT = 56 * 1024 * 1024
S5_T = 16
S5_BLK = S5_T * LANE

C_CKV = 0
C_CQ = 128
C_KR = 384
C_S5 = 512
C_MIX = 768
IN_PAD = 2560
MIX_W = 3 * BRANCH + 4 * BRANCH


def _dot(a, b):
    return jnp.dot(a, b, preferred_element_type=F32)


def _dot_nt(a, b):
    return lax.dot_general(a, b, (((1,), (1,)), ((), ())), preferred_element_type=F32)


def _ln_rows(x):
    mu = jnp.mean(x, axis=-1, keepdims=True)
    xc = x - mu
    var = jnp.mean(xc * xc, axis=-1, keepdims=True)
    return xc * lax.rsqrt(var + LN_EPS)


def _sigmoid(x):
    return 0.5 * jnp.tanh(0.5 * x) + 0.5


def _mod_kernel(c_ref, w_ref, b_ref, o_ref):
    c = c_ref[...]
    s = c * _sigmoid(c)
    o_ref[0] = jnp.dot(s, w_ref[0], preferred_element_type=F32,
                       precision=lax.Precision.HIGHEST) + b_ref[0]


def _modulation(cc, w_mod, b_mod):
    L, D, D3 = w_mod.shape
    nb = D3 // D
    return pl.pallas_call(
        _mod_kernel,
        out_shape=jax.ShapeDtypeStruct((L, 8, D3), F32),
        grid=(L, nb),
        in_specs=[pl.BlockSpec((8, D), lambda l, j: (0, 0)),
                  pl.BlockSpec((1, D, D), lambda l, j: (l, 0, j)),
                  pl.BlockSpec((1, 1, D), lambda l, j: (l, 0, j))],
        out_specs=pl.BlockSpec((1, 8, D), lambda l, j: (l, 0, j)),
        compiler_params=pltpu.CompilerParams(vmem_limit_bytes=V7X_VMEM_LIMIT),
        name="modulation",
    )(cc, w_mod, b_mod)


def _rope_lanes(x, c, s1, s2):
    w = x.shape[-1]
    return x * c + pltpu.roll(x, w - 8, 1) * s1 + pltpu.roll(x, 8, 1) * s2


def _inproj_kernel(x_ref, shift_ref, scale_ref, w_ref, gkv_ref, gq_ref, wk_ref, wv_ref, wq_ref,
                   pk_ref, vone_ref, rc_ref, rs1_ref, rs2_ref,
                   q_ref, k_ref, vt_ref, s5_ref, mix_ref):
    x = x_ref[0]
    h = _ln_rows(x) * (1.0 + scale_ref[0]) + shift_ref[0]
    proj = _dot(h.astype(BF16), w_ref[...])
    ckv = proj[:, C_CKV:C_CKV + KV_LORA]
    cq = proj[:, C_CQ:C_CQ + Q_LORA]
    kr = proj[:, C_KR:C_KR + LANE]

    rk = ckv * lax.rsqrt(jnp.mean(ckv * ckv, axis=-1, keepdims=True) + LN_EPS) * gkv_ref[...]
    rk = rk.astype(BF16)
    rq = cq * lax.rsqrt(jnp.mean(cq * cq, axis=-1, keepdims=True) + LN_EPS) * gq_ref[...]
    rq = rq.astype(BF16)

    rc, rs1, rs2 = rc_ref[...], rs1_ref[...], rs2_ref[...]
    krp = _rope_lanes(_dot(kr.astype(BF16), pk_ref[...]), rc, rs1, rs2)
    kn = _dot(rk, wk_ref[...])
    vv = _dot(rk, wv_ref[...]) + vone_ref[...]
    qq = _dot(rq, wq_ref[...])
    for hd in range(N_HEADS):
        sl = slice(hd * HEAD_PAD, (hd + 1) * HEAD_PAD)
        q_ref[0, hd] = _rope_lanes(qq[:, sl], rc, rs1, rs2).astype(BF16)
        k_ref[0, hd] = (kn[:, sl] + krp).astype(BF16)
        vt_ref[0, hd, 0] = vv[:, sl].T[:V_ROWS].astype(BF16)
    s5_ref[0] = proj[:, C_S5:C_S5 + BRANCH].astype(BF16)
    mix_ref[0] = proj[:, C_MIX:C_MIX + MIX_W].astype(BF16)


def _inproj(x, shift, scale, batch_mod, wts, tables, tm):
    B, n, D = x.shape
    w_in, gkv, gq, wk, wv, wq, pk, vone = wts
    rc, rs1, rs2 = tables
    mod_map = (lambda b, i: (b, 0, 0)) if batch_mod else (lambda b, i: (0, 0, 0))
    full2 = lambda b, i: (0, 0)
    tab = pl.BlockSpec((tm, LANE), lambda b, i: (i, 0))
    hw = N_HEADS * HEAD_PAD
    return pl.pallas_call(
        _inproj_kernel,
        out_shape=(jax.ShapeDtypeStruct((B, N_HEADS, n, HEAD_PAD), BF16),
                   jax.ShapeDtypeStruct((B, N_HEADS, n, HEAD_PAD), BF16),
                   jax.ShapeDtypeStruct((B, N_HEADS, n // tm, V_ROWS, tm), BF16),
                   jax.ShapeDtypeStruct((B, n, BRANCH), BF16),
                   jax.ShapeDtypeStruct((B, n, MIX_W), BF16)),
        grid=(B, n // tm),
        in_specs=[pl.BlockSpec((1, tm, D), lambda b, i: (b, i, 0)),
                  pl.BlockSpec((1, 1, D), mod_map),
                  pl.BlockSpec((1, 1, D), mod_map),
                  pl.BlockSpec((D, IN_PAD), full2),
                  pl.BlockSpec((1, KV_LORA), full2),
                  pl.BlockSpec((1, Q_LORA), full2),
                  pl.BlockSpec((KV_LORA, hw), full2),
                  pl.BlockSpec((KV_LORA, hw), full2),
                  pl.BlockSpec((Q_LORA, hw), full2),
                  pl.BlockSpec((LANE, LANE), full2),
                  pl.BlockSpec((1, hw), full2),
                  tab, tab, tab],
        out_specs=(pl.BlockSpec((1, N_HEADS, tm, HEAD_PAD), lambda b, i: (b, 0, i, 0)),
                   pl.BlockSpec((1, N_HEADS, tm, HEAD_PAD), lambda b, i: (b, 0, i, 0)),
                   pl.BlockSpec((1, N_HEADS, 1, V_ROWS, tm), lambda b, i: (b, 0, i, 0, 0)),
                   pl.BlockSpec((1, tm, BRANCH), lambda b, i: (b, i, 0)),
                   pl.BlockSpec((1, tm, MIX_W), lambda b, i: (b, i, 0))),
        compiler_params=pltpu.CompilerParams(
            dimension_semantics=("parallel", "parallel"), vmem_limit_bytes=V7X_VMEM_LIMIT),
        name="inproj",
    )(x, shift, scale, w_in, gkv, gq, wk, wv, wq, pk, vone, rc, rs1, rs2)


ATT_TK = 512


def _attn_kernel(*refs, n_src):
    q_ref = refs[0]
    k_refs = refs[1:1 + n_src]
    vt_refs = refs[1 + n_src:1 + 2 * n_src]
    o_ref = refs[1 + 2 * n_src]
    s_scr, pa_scr, pb_scr, acc_scr, m_scr = refs[2 + 2 * n_src:]
    tq = q_ref.shape[2]
    step = pl.program_id(1)

    @pl.when(step == 0)
    def _():
        s_scr[...] = jnp.zeros(s_scr.shape, F32)
        m_scr[...] = jnp.zeros(m_scr.shape, F32)

    q = q_ref[0, 0]
    m_prev = jnp.max(m_scr[...], axis=0, keepdims=True)
    acc_scr[...] = jnp.zeros(acc_scr.shape, F32)
    m_acc = jnp.full((8, tq), -jnp.inf, F32)

    tiles = []
    row = 0
    for si in range(n_src):
        nt, tk = vt_refs[si].shape[2], vt_refs[si].shape[4]
        for j in range(nt):
            tiles.append((si, j, row, tk))
            row += tk
    p_bufs = ((pa_scr, 0), (pa_scr, 1), (pb_scr, 0), (pb_scr, 1))

    def ex(i):
        si, j, r, tk = tiles[i]
        buf, slot = p_bufs[i % len(p_bufs)]
        buf[slot, 0:tk, :] = jnp.exp2((s_scr[r:r + tk, :] - m_prev).astype(BF16))

    def qk(i, m):
        si, j, r, tk = tiles[i]
        s = _dot_nt(k_refs[si][0, 0, j * tk:(j + 1) * tk, :], q)
        s_scr[r:r + tk, :] = s
        parts = [s[a:a + 8, :] for a in range(0, tk, 8)]
        while len(parts) > 1:
            parts = [jnp.maximum(parts[a], parts[a + 1]) for a in range(0, len(parts), 2)]
        return jnp.maximum(m, parts[0])

    def mm(i):
        si, j, r, tk = tiles[i]
        buf, slot = p_bufs[i % len(p_bufs)]
        acc_scr[...] += _dot(vt_refs[si][0, 0, j], buf[slot, 0:tk, :])

    ex(0)
    for i in range(len(tiles)):
        if i + 1 < len(tiles):
            ex(i + 1)
        m_acc = qk(i, m_acc)
        mm(i)

    m_scr[...] = m_acc

    @pl.when(step > 0)
    def _():
        acc = acc_scr[...]
        o_ref[0, 0] = (acc[:V_DIM, :] / acc[V_DIM:V_DIM + 1, :]).astype(o_ref.dtype)


def _attention(q, ks, vts, tq):
    B, H, n, _ = q.shape
    n_src = len(ks)
    nkeys = sum(k.shape[2] for k in ks)
    tk_max = max(v.shape[4] for v in vts)
    for k, v in zip(ks, vts):
        nt = v.shape[2]
        assert k.shape[2] == nt * v.shape[4]
    nq = n // tq
    n_items = H * nq
    cur = lambda k: jnp.minimum(k, n_items - 1)
    prv = lambda k: jnp.maximum(k - 1, 0)
    k_spec = lambda a: pl.BlockSpec((1, 1) + a.shape[2:], lambda b, k: (b, cur(k) // nq, 0, 0))
    vt_spec = lambda a: pl.BlockSpec((1, 1) + a.shape[2:], lambda b, k: (b, prv(k) // nq, 0, 0, 0))
    return pl.pallas_call(
        functools.partial(_attn_kernel, n_src=n_src),
        out_shape=jax.ShapeDtypeStruct((B, H, V_DIM, n), BF16),
        grid=(B, n_items + 1),
        in_specs=[pl.BlockSpec((1, 1, tq, HEAD_PAD), lambda b, k: (b, cur(k) // nq, cur(k) % nq, 0))]
                 + [k_spec(k) for k in ks] + [vt_spec(v) for v in vts],
        out_specs=pl.BlockSpec((1, 1, V_DIM, tq), lambda b, k: (b, prv(k) // nq, 0, prv(k) % nq)),
        scratch_shapes=[pltpu.VMEM((nkeys, tq), F32),
                        pltpu.VMEM((2, tk_max, tq), BF16), pltpu.VMEM((2, tk_max, tq), BF16),
                        pltpu.VMEM((V_ROWS, tq), F32), pltpu.VMEM((8, tq), F32)],
        compiler_params=pltpu.CompilerParams(
            dimension_semantics=("parallel", "arbitrary"), vmem_limit_bytes=V7X_VMEM_LIMIT),
        name="attention",
    )(q, *ks, *vts)


def _s5_kernel(ul_ref, uc_ref, m_ref, pb_ref, pc_ref, adec_ref, yl_ref, yc_ref,
               ut_ref, sfr_ref, sbr_ref, sfi_ref, sbi_ref, f_ref):
    n = ul_ref.shape[1]
    nc = uc_ref.shape[1]
    kl = n // S5_T
    kc = nc // S5_T
    nblk = n // S5_BLK
    rows_blk = S5_BLK // S5_T
    ncol = ut_ref.shape[2]

    half_g = S5_G // 2

    def relayout_in(nrows, col0):
        for s in range(S5_T):
            for h in range(2):
                blk = f_ref[h, pl.ds(s, nrows, stride=S5_T), :]
                if nrows < LANE:
                    blk = jnp.concatenate([blk, jnp.zeros((LANE - nrows, LANE), F32)], axis=0)
                bt = blk.T
                for gg in range(half_g):
                    ut_ref[h * half_g + gg, s * S5_H:(s + 1) * S5_H, col0:col0 + LANE] = (
                        bt[gg * S5_H:(gg + 1) * S5_H, :].astype(BF16))

    for jb in range(nblk):
        for h in range(2):
            f_ref[h] = ul_ref[0, jb * S5_BLK:(jb + 1) * S5_BLK, h * LANE:(h + 1) * LANE].astype(F32)
        relayout_in(rows_blk, jb * LANE)
    for h in range(2):
        f_ref[h, 0:nc, :] = uc_ref[0, :, h * LANE:(h + 1) * LANE].astype(F32)
    relayout_in(kc, kl)

    comp_refs = (sfr_ref, sbr_ref, sfi_ref, sbi_ref)
    for pair in range(S5_G // 2):
        st0 = _dot(pb_ref[2 * pair], ut_ref[2 * pair])
        st1 = _dot(pb_ref[2 * pair + 1], ut_ref[2 * pair + 1])
        for c, ref in enumerate(comp_refs):
            both = jnp.concatenate([st0[c * S5_P:(c + 1) * S5_P], st1[c * S5_P:(c + 1) * S5_P]], axis=0)
            ref[:, pair * LANE:(pair + 1) * LANE] = both.T

    afr, afi, abr, abi = (adec_ref[c:c + 1, :] for c in range(4))

    def step(rf, rb, carry):
        fr, fi, br, bi = carry
        s_fr, s_fi = sfr_ref[pl.ds(rf, 1), :], sfi_ref[pl.ds(rf, 1), :]
        s_br, s_bi = sbr_ref[pl.ds(rb, 1), :], sbi_ref[pl.ds(rb, 1), :]
        sfr_ref[pl.ds(rf, 1), :] = fr
        sfi_ref[pl.ds(rf, 1), :] = fi
        sbr_ref[pl.ds(rb, 1), :] = br
        sbi_ref[pl.ds(rb, 1), :] = bi
        return (afr * fr - afi * fi + s_fr, afr * fi + afi * fr + s_fi,
                abr * br - abi * bi + s_br, abr * bi + abi * br + s_bi)

    zero = jnp.zeros((1, S5_G * S5_P), F32)
    carry = lax.fori_loop(0, kc, lambda i, c: step(kl + i, kl + kc - 1 - i, c), (zero,) * 4)
    lax.fori_loop(0, kl, lambda i, c: step(i, kl - 1 - i, c), carry)

    for pair in range(S5_G // 2):
        comp_t = [ref[:, pair * LANE:(pair + 1) * LANE].T for ref in comp_refs]
        for k in range(2):
            g = 2 * pair + k
            hin_t = jnp.concatenate([t[k * S5_P:(k + 1) * S5_P] for t in comp_t], axis=0).astype(BF16)
            y = _dot(m_ref[g], ut_ref[g]) + _dot(pc_ref[g], hin_t)
            ut_ref[g] = y.astype(BF16)

    def relayout_out(nrows, col0):
        for t in range(S5_T):
            for h in range(2):
                yt = jnp.concatenate(
                    [ut_ref[h * half_g + gg, t * S5_H:(t + 1) * S5_H, col0:col0 + LANE]
                     for gg in range(half_g)], axis=0).astype(F32)
                f_ref[h, pl.ds(t, nrows, stride=S5_T), :] = yt.T[:nrows]

    for jb in range(nblk):
        relayout_out(rows_blk, jb * LANE)
        for h in range(2):
            yl_ref[0, jb * S5_BLK:(jb + 1) * S5_BLK, h * LANE:(h + 1) * LANE] = (
                f_ref[h].astype(yl_ref.dtype))
    relayout_out(kc, kl)
    for h in range(2):
        yc_ref[0, :, h * LANE:(h + 1) * LANE] = f_ref[h, 0:nc, :].astype(yc_ref.dtype)


def _s5(ul, uc, ops):
    B, n, _ = ul.shape
    nc = uc.shape[1]
    m, pb, pc, adec = ops
    assert n % S5_BLK == 0 and nc % S5_T == 0 and nc // S5_T <= LANE and nc <= S5_BLK
    assert (n // S5_T) % 2 == 0 and (nc // S5_T) % 2 == 0
    ncol = n // S5_T + LANE
    w3 = lambda b: (0, 0, 0)
    return pl.pallas_call(
        _s5_kernel,
        out_shape=(jax.ShapeDtypeStruct((B, n, BRANCH), BF16),
                   jax.ShapeDtypeStruct((B, nc, BRANCH), BF16)),
        grid=(B,),
        in_specs=[pl.BlockSpec((1, n, BRANCH), lambda b: (b, 0, 0)),
                  pl.BlockSpec((1, nc, BRANCH), lambda b: (b, 0, 0)),
                  pl.BlockSpec(m.shape, w3), pl.BlockSpec(pb.shape, w3), pl.BlockSpec(pc.shape, w3),
                  pl.BlockSpec(adec.shape, lambda b: (0, 0))],
        out_specs=(pl.BlockSpec((1, n, BRANCH), lambda b: (b, 0, 0)),
                   pl.BlockSpec((1, nc, BRANCH), lambda b: (b, 0, 0))),
        scratch_shapes=[pltpu.VMEM((S5_G, S5_T * S5_H, ncol), BF16),
                        pltpu.VMEM((ncol, S5_G * S5_P), F32), pltpu.VMEM((ncol, S5_G * S5_P), F32),
                        pltpu.VMEM((ncol, S5_G * S5_P), F32), pltpu.VMEM((ncol, S5_G * S5_P), F32),
                        pltpu.VMEM((2, S5_BLK, LANE), F32)],
        compiler_params=pltpu.CompilerParams(
            dimension_semantics=("parallel",), vmem_limit_bytes=V7X_VMEM_LIMIT),
        name="s5",
    )(ul, uc, m, pb, pc, adec)


POOL_HALO = 16


def _gelu_tanh(x):
    return 0.5 * x * (1.0 + jnp.tanh(math.sqrt(2.0 / math.pi) * (x + 0.044715 * (x * x * x))))


def _merge_kernel(x_ref, att_ref, y5_ref, mix_ref, prev_ref, next_ref, gate_ref,
                  wpool_ref, pscale_ref, wglu_ref, bglu_ref, sg_ref, sb_ref, ws_ref, bs_ref,
                  wout_ref, lng_ref, lnb_ref, o_ref, a_ref, b2_ref, b4_ref, b8_ref,
                  *, n_total, alpha):
    tm = x_ref.shape[1]
    i = pl.program_id(1)
    ni = pl.num_programs(1)
    hl = POOL_HALO

    pool_in = mix_ref[0, :, 0:BRANCH].astype(F32)
    a_ref[0:hl, :] = jnp.where(i > 0, prev_ref[0].astype(F32), 0.0)
    a_ref[hl:hl + tm, :] = pool_in
    a_ref[hl + tm:hl + tm + hl, :] = jnp.where(i < ni - 1, next_ref[0].astype(F32), 0.0)
    r2 = tm + 16
    b2_ref[8:8 + r2, :] = a_ref[7:7 + r2, :] + a_ref[8:8 + r2, :]
    r4 = tm + 12
    b4_ref[10:10 + r4, :] = b2_ref[9:9 + r4, :] + b2_ref[11:11 + r4, :]
    r8 = tm + 8
    b8_ref[12:12 + r8, :] = b4_ref[10:10 + r8, :] + b4_ref[14:14 + r8, :]
    c2 = b2_ref[hl:hl + tm, :]
    c4 = b4_ref[hl:hl + tm, :]
    c8 = b8_ref[hl:hl + tm, :]
    c16 = b8_ref[hl - 4:hl - 4 + tm, :] + b8_ref[hl + 4:hl + 4 + tm, :]
    lane = lax.broadcasted_iota(jnp.int32, (tm, BRANCH), 1)
    grp = lane // POOL_GC
    tot = jnp.where(grp == 0, c2, jnp.where(grp == 1, c4, jnp.where(grp == 2, c8, c16)))
    t = lax.broadcasted_iota(jnp.int32, (tm, BRANCH), 0) + i * tm
    half = jnp.left_shift(1, grp)
    cnt = jnp.minimum(t + half, n_total) - jnp.maximum(t - half, 0)
    pooled = tot / cnt.astype(F32) - pool_in
    pool_l = _dot(pooled.astype(BF16), wpool_ref[...]) * pscale_ref[...]

    g1 = _gelu_tanh(y5_ref[0].astype(F32))
    s5_l = g1 * _sigmoid(_dot(g1.astype(BF16), wglu_ref[...]) + bglu_ref[...])

    su = mix_ref[0, :, BRANCH:2 * BRANCH].astype(F32)
    sv = mix_ref[0, :, 2 * BRANCH:3 * BRANCH].astype(F32)
    vn = (_ln_rows(sv) * sg_ref[...] + sb_ref[...]).astype(BF16)
    lane_c = lax.broadcasted_iota(jnp.int32, (CHUNK, BRANCH), 1) // SGU_HD
    mixed = []
    for c in range(tm // CHUNK):
        vc = vn[c * CHUNK:(c + 1) * CHUNK, :]
        r = _dot(ws_ref[SGU_HEADS - 1], vc)
        for hd in range(SGU_HEADS - 2, -1, -1):
            r = jnp.where(lane_c == hd, _dot(ws_ref[hd], vc), r)
        mixed.append(r + bs_ref[...])
    sgu_l = su * jnp.concatenate(mixed, axis=0)

    gts = mix_ref[0, :, 3 * BRANCH:].astype(F32)
    att = att_ref[0].astype(F32).reshape(N_HEADS * V_DIM, tm).T
    cat = jnp.concatenate([att, pool_l, s5_l, sgu_l], axis=-1)
    y = _dot((cat * (gts * _sigmoid(gts))).astype(BF16), wout_ref[...])
    z = alpha * x_ref[0] + gate_ref[0] * y
    o_ref[0] = _ln_rows(z) * lng_ref[...] + lnb_ref[...]


def _merge(x, att, y5, mix, gate, batch_mod, wts, tm, alpha):
    B, n, D = x.shape
    wpool, pscale, wglu, bglu, sg, sb, ws, bs, wout, lng, lnb = wts
    hb = tm // POOL_HALO
    nhb = n // POOL_HALO
    full2 = lambda b, i: (0, 0)
    mod_map = (lambda b, i: (b, 0, 0)) if batch_mod else (lambda b, i: (0, 0, 0))
    row = lambda w: pl.BlockSpec((1, w), full2)
    return pl.pallas_call(
        functools.partial(_merge_kernel, n_total=n, alpha=alpha),
        out_shape=jax.ShapeDtypeStruct((B, n, D), F32),
        grid=(B, n // tm),
        in_specs=[pl.BlockSpec((1, tm, D), lambda b, i: (b, i, 0)),
                  pl.BlockSpec((1, N_HEADS, V_DIM, tm), lambda b, i: (b, 0, 0, i)),
                  pl.BlockSpec((1, tm, BRANCH), lambda b, i: (b, i, 0)),
                  pl.BlockSpec((1, tm, MIX_W), lambda b, i: (b, i, 0)),
                  pl.BlockSpec((1, POOL_HALO, BRANCH),
                               lambda b, i: (b, jnp.maximum(i * hb - 1, 0), 0)),
                  pl.BlockSpec((1, POOL_HALO, BRANCH),
                               lambda b, i: (b, jnp.minimum((i + 1) * hb, nhb - 1), 0)),
                  pl.BlockSpec((1, 1, D), mod_map),
                  pl.BlockSpec((BRANCH, BRANCH), full2), row(BRANCH),
                  pl.BlockSpec((BRANCH, BRANCH), full2), row(BRANCH),
                  row(BRANCH), row(BRANCH),
                  pl.BlockSpec((SGU_HEADS, CHUNK, CHUNK), lambda b, i: (0, 0, 0)),
                  pl.BlockSpec((CHUNK, BRANCH), full2),
                  pl.BlockSpec((4 * BRANCH, D), full2), row(D), row(D)],
        out_specs=pl.BlockSpec((1, tm, D), lambda b, i: (b, i, 0)),
        scratch_shapes=[pltpu.VMEM((tm + 2 * POOL_HALO, BRANCH), F32) for _ in range(4)],
        compiler_params=pltpu.CompilerParams(
            dimension_semantics=("parallel", "parallel"), vmem_limit_bytes=V7X_VMEM_LIMIT),
        name="merge",
    )(x, att, y5, mix, mix, mix, gate, wpool, pscale, wglu, bglu, sg, sb, ws, bs, wout, lng, lnb)


def _prep_inproj_weights(w_in, g_q, w_uq, g_kv, w_ukv):
    D = w_in.shape[0]
    o_kr = KV_LORA
    o_s5 = o_kr + ROPE
    o_cq = o_s5 + BRANCH
    o_rest = o_cq + Q_LORA
    wb = w_in.astype(BF16)
    zpad = lambda k: jnp.zeros((D, k), BF16)
    w = jnp.concatenate(
        [wb[:, 0:KV_LORA], wb[:, o_cq:o_cq + Q_LORA], zpad(C_KR - C_CQ - Q_LORA),
         wb[:, o_kr:o_kr + ROPE], zpad(C_S5 - C_KR - ROPE), wb[:, o_s5:o_s5 + BRANCH],
         wb[:, o_rest:]], axis=1)
    assert w.shape == (D, IN_PAD)
    hw = N_HEADS * HEAD_PAD
    scale = (NOPE + ROPE) ** -0.5 * math.log2(math.e)
    ukv = w_ukv.reshape(KV_LORA, N_HEADS, NOPE + V_DIM)
    uq = w_uq.reshape(Q_LORA, N_HEADS, NOPE + ROPE) * scale
    padl = lambda a: jnp.pad(a, ((0, 0), (0, 0), (0, HEAD_PAD - a.shape[2])))
    wk = padl(ukv[:, :, :NOPE])
    wv = padl(ukv[:, :, NOPE:])
    wq = padl(uq)
    pk = (jnp.arange(LANE)[:, None] + NOPE == jnp.arange(LANE)[None, :]) & (jnp.arange(LANE)[:, None] < ROPE)
    vone = jnp.tile(jnp.arange(HEAD_PAD) == V_DIM, N_HEADS).astype(F32)
    return (w, g_kv.reshape(1, KV_LORA), g_q.reshape(1, Q_LORA),
            wk.reshape(KV_LORA, hw).astype(BF16), wv.reshape(KV_LORA, hw).astype(BF16),
            wq.reshape(Q_LORA, hw).astype(BF16), pk.astype(BF16), vone.reshape(1, hw))


def _rope_tables(n, rotate):
    ones = jnp.ones((n, NOPE), F32)
    zpad = jnp.zeros((n, HEAD_PAD - NOPE - ROPE), F32)
    if not rotate:
        c = jnp.concatenate([ones, jnp.ones((n, ROPE), F32), zpad], axis=1)
        z = jnp.zeros((n, HEAD_PAD), F32)
        return c, z, z
    nf = ROPE // 4
    rows = n // GRID_W
    lane = np.arange(HEAD_PAD)
    blk = (lane - NOPE) // (2 * nf)
    in_rope = (lane >= NOPE) & (lane < NOPE + ROPE)
    first = ((lane - NOPE) % (2 * nf)) < nf
    inv = ROPE_BASE ** (-jnp.arange(nf, dtype=F32) / nf)
    inv_lane = jnp.tile(inv, HEAD_PAD // nf)
    is_row = jnp.asarray(in_rope & (blk == 0))
    is_col = jnp.asarray(in_rope & (blk == 1))
    ang_r = jnp.arange(rows, dtype=F32)[:, None] * inv_lane[None, :]
    ang_c = jnp.arange(GRID_W, dtype=F32)[:, None] * inv_lane[None, :]

    def grid(fr, fc, base):
        t3 = (jnp.where(is_row, fr, 0.0)[:, None, :] + jnp.where(is_col, fc, 0.0)[None, :, :]
              + base[None, None, :])
        return t3.reshape(n, HEAD_PAD)

    nope_one = jnp.asarray((lane < NOPE).astype(np.float32))
    zero = jnp.zeros((HEAD_PAD,), F32)
    m1 = jnp.asarray(first)
    c = grid(jnp.cos(ang_r), jnp.cos(ang_c), nope_one)
    s1 = grid(jnp.where(m1, -jnp.sin(ang_r), 0.0), jnp.where(m1, -jnp.sin(ang_c), 0.0), zero)
    s2 = grid(jnp.where(m1, 0.0, jnp.sin(ang_r)), jnp.where(m1, 0.0, jnp.sin(ang_c)), zero)
    return c, s1, s2


def _s5_selectors():
    T, H = S5_T, S5_H
    nslot = 2 * T - 1
    toep = np.zeros((T, nslot * H, T * H), np.float32)
    for t in range(T):
        for s in range(T):
            slot = 0 if t == s else (t - s if t > s else T - 1 + s - t)
            for h in range(H):
                toep[t, slot * H + h, s * H + h] = 1.0
    e_s = np.kron(np.eye(T, dtype=np.float32), np.ones((1, H), np.float32))
    e_h = np.kron(np.ones((1, T), np.float32), np.eye(H, dtype=np.float32))
    return toep, e_s, e_h


def _prep_s5(lam_re, lam_im, log_dt, b_re, b_im, c_re, c_im, s5_d):
    T, G, P, H = S5_T, S5_G, S5_P, S5_H
    hi = lax.Precision.HIGHEST
    toep, e_s, e_h = _s5_selectors()
    lam_re = lam_re.astype(F32)
    lam_im = lam_im.astype(F32)
    dt = jnp.exp(log_dt.astype(F32))[..., None]
    zr, zi = lam_re * dt, lam_im * dt
    j = jnp.arange(T + 1, dtype=F32)[None, None, :, None]
    mag = jnp.exp(zr[:, :, None, :] * j)
    ang = zi[:, :, None, :] * j
    pr, pi = mag * jnp.cos(ang), mag * jnp.sin(ang)
    nr, ni = pr[:, :, 1] - 1.0, pi[:, :, 1]
    den = lam_re * lam_re + lam_im * lam_im
    k_re = (nr * lam_re + ni * lam_im) / den
    k_im = (ni * lam_re - nr * lam_im) / den
    bb_re = k_re[..., None] * b_re - k_im[..., None] * b_im
    bb_im = k_re[..., None] * b_im + k_im[..., None] * b_re
    cre, cim = c_re[:, :, None, :, :], c_im[:, :, None, :, :]
    prb, pib = pr[:, :, :, None, :], pi[:, :, :, None, :]
    cl_re = cre * prb - cim * pib
    cl_im = cre * pib + cim * prb
    kk = (jnp.einsum('dgjop,dgph->dgjoh', cl_re[:, :, :T], bb_re, precision=hi)
          - jnp.einsum('dgjop,dgph->dgjoh', cl_im[:, :, :T], bb_im, precision=hi))
    skip = jnp.eye(H, dtype=F32)[None] * s5_d.astype(F32).reshape(G, H, 1)
    x = jnp.concatenate([(kk[0, :, 0] + kk[1, :, 0] + skip)[:, None], kk[0, :, 1:], kk[1, :, 1:]],
                        axis=1)
    x = x.transpose(0, 2, 1, 3).reshape(G, H, (2 * T - 1) * H)
    m = jnp.einsum('goJ,tJc->gtoc', x.astype(BF16), jnp.asarray(toep, BF16),
                   preferred_element_type=F32).reshape(G, T * H, T * H)
    over_s = lambda a: jnp.einsum('gap,ac->gpc', a, jnp.asarray(e_s), precision=hi)
    over_h = lambda a: jnp.einsum('gpa,ac->gpc', a, jnp.asarray(e_h), precision=hi)

    def lb(p_r, p_i, d):
        ar, ai, br, bi = over_s(p_r), over_s(p_i), over_h(bb_re[d]), over_h(bb_im[d])
        return ar * br - ai * bi, ar * bi + ai * br

    f_re, f_im = lb(pr[0, :, T - 1::-1], pi[0, :, T - 1::-1], 0)
    b_re_, b_im_ = lb(pr[1, :, :T], pi[1, :, :T], 1)
    pb = jnp.concatenate([f_re, b_re_, f_im, b_im_], axis=1)
    to_c = lambda a: a.reshape(G, T * H, P)
    zf_re, zf_im = cl_re[0, :, 1:], cl_im[0, :, 1:]
    zb_re, zb_im = cl_re[1, :, T:0:-1], cl_im[1, :, T:0:-1]
    pc = jnp.concatenate([to_c(zf_re), to_c(zb_re), -to_c(zf_im), -to_c(zb_im)], axis=2)
    adec = jnp.stack([pr[0, :, T], pi[0, :, T], pr[1, :, T], pi[1, :, T]], axis=0).reshape(4, G * P)
    return (m.astype(BF16), pb.astype(BF16), pc.astype(BF16), adec)


def _prep_merge_weights(w_pool, pool_scale, w_glu, b_glu, sgu_g, sgu_b, w_s, b_s, w_out, ln_g, ln_b):
    z = jnp.zeros((POOL_GC, POOL_GC), F32)
    ng = len(POOL_WINDOWS)
    wpool = jnp.concatenate(
        [jnp.concatenate([w_pool[gi] if gj == gi else z for gj in range(ng)], axis=1)
         for gi in range(ng)], axis=0)
    bs = jnp.repeat(b_s.T, SGU_HD, axis=1)
    r = lambda a: a.reshape(1, -1).astype(F32)
    return (wpool.astype(BF16), r(pool_scale), w_glu.astype(BF16), r(b_glu), r(sgu_g), r(sgu_b),
            w_s.astype(BF16), bs.astype(F32), w_out.astype(BF16), r(ln_g), r(ln_b))


def kernel(x, c, ctx, c_ctx, w_mod, b_mod, w_in, g_q, w_uq, g_kv, w_ukv, w_pool, pool_scale, lam_re, lam_im, log_dt, s5_b_re, s5_b_im, s5_c_re, s5_c_im, s5_d, w_glu, b_glu, sgu_g, sgu_b, w_s, b_s, w_out, ln_g, ln_b):
    B, n, D = x.shape
    nc = ctx.shape[1]
    depth = w_mod.shape[0]
    alpha = (2 * depth) ** 0.25
    assert B + 1 <= 8
    tm = min(ATT_TK, n)
    tmc = min(ATT_TK, nc)
    tq = min(512, n)
    tqc = min(512, nc)

    cc = jnp.concatenate([c.astype(F32), c_ctx.astype(F32).reshape(1, D),
                          jnp.zeros((8 - B - 1, D), F32)], axis=0)
    mod = _modulation(cc, w_mod, b_mod.reshape(depth, 1, 3 * D))
    tab_l = _rope_tables(n, True)
    tab_c = _rope_tables(nc, False)

    wts_in_all = jax.vmap(_prep_inproj_weights)(w_in, g_q, w_uq, g_kv, w_ukv)
    s5_ops_all = jax.vmap(_prep_s5)(lam_re, lam_im, log_dt, s5_b_re, s5_b_im, s5_c_re, s5_c_im, s5_d)
    wts_mg_all = jax.vmap(_prep_merge_weights)(w_pool, pool_scale, w_glu, b_glu, sgu_g, sgu_b,
                                               w_s, b_s, w_out, ln_g, ln_b)

    for l in range(depth):
        last = l == depth - 1
        shift = mod[l, :, 0:D].reshape(8, 1, D)
        scale = mod[l, :, D:2 * D].reshape(8, 1, D)
        gate = mod[l, :, 2 * D:].reshape(8, 1, D)
        wts_in = tuple(a[l] for a in wts_in_all)
        s5_ops = tuple(a[l] for a in s5_ops_all)
        wts_mg = tuple(a[l] for a in wts_mg_all)

        q_l, k_l, v_l, s5u_l, mix_l = _inproj(x, shift[:B], scale[:B], True, wts_in, tab_l, tm)
        q_c, k_c, v_c, s5u_c, mix_c = _inproj(ctx, shift[B:B + 1], scale[B:B + 1], False, wts_in, tab_c, tmc)
        att_l = _attention(q_l, [k_l, k_c], [v_l, v_c], tq)
        y5_l, y5_c = _s5(s5u_l, s5u_c, s5_ops)
        x_new = _merge(x, att_l, y5_l, mix_l, gate[:B], True, wts_mg, tm, alpha)
        if not last:
            att_c = _attention(q_c, [k_c], [v_c], tqc)
            ctx = _merge(ctx, att_c, y5_c, mix_c, gate[B:B + 1], False, wts_mg, tmc, alpha)
        x = x_new
    return x
```

```python
import functools
import math

import jax
import jax.numpy as jnp
import numpy as np
from jax import lax
from jax.experimental import pallas as pl
from jax.experimental.pallas import tpu as pltpu

F32 = jnp.float32
BF16 = jnp.bfloat16

GRID_W = 64
N_HEADS = 4
NOPE = 64
ROPE = 32
V_DIM = 64
Q_LORA = 192
KV_LORA = 128
ROPE_BASE = 10000.0
BRANCH = 256
POOL_WINDOWS = (2, 4, 8, 16)
POOL_GC = 64
S5_H = 16
S5_G = 16
S5_P = 64
SGU_HEADS = 4
SGU_HD = 64
CHUNK = 128
LN_EPS = 1e-6

LANE = 128
HEAD_PAD = 128
V_ROWS = 80
V7X_VMEM_LIMIT = 56 * 1024 * 1024
S5_T = 16
S5_BLK = S5_T * LANE

C_CKV = 0
C_CQ = 128
C_KR = 384
C_S5 = 512
C_MIX = 768
IN_PAD = 2560
INPROJ_SUB = 128
MIX_W = 3 * BRANCH + 4 * BRANCH


def _dot(a, b):
    return jnp.dot(a, b, preferred_element_type=F32)


def _dot_nt(a, b):
    return lax.dot_general(a, b, (((1,), (1,)), ((), ())), preferred_element_type=F32)


def _ln_rows(x):
    mu = jnp.mean(x, axis=-1, keepdims=True)
    xc = x - mu
    var = jnp.mean(xc * xc, axis=-1, keepdims=True)
    return xc * lax.rsqrt(var + LN_EPS)


def _sigmoid(x):
    return 0.5 * jnp.tanh(0.5 * x) + 0.5


def _silu(x):
    h = 0.5 * x
    return h + h * jnp.tanh(h)


def _mod_kernel(c_ref, w_ref, b_ref, o_ref):
    c = c_ref[...]
    s = c * _sigmoid(c)
    o_ref[0] = jnp.dot(s, w_ref[0], preferred_element_type=F32,
                       precision=lax.Precision.HIGHEST) + b_ref[0]


def _modulation(cc, w_mod, b_mod):
    L, D, D3 = w_mod.shape
    nb = D3 // D
    return pl.pallas_call(
        _mod_kernel,
        out_shape=jax.ShapeDtypeStruct((L, 8, D3), F32),
        grid=(L, nb),
        in_specs=[pl.BlockSpec((8, D), lambda l, j: (0, 0)),
                  pl.BlockSpec((1, D, D), lambda l, j: (l, 0, j)),
                  pl.BlockSpec((1, 1, D), lambda l, j: (l, 0, j))],
        out_specs=pl.BlockSpec((1, 8, D), lambda l, j: (l, 0, j)),
        compiler_params=pltpu.CompilerParams(vmem_limit_bytes=V7X_VMEM_LIMIT),
        name="modulation",
    )(cc, w_mod, b_mod)


def _rope_lanes(x, c, s1, s2):
    w = x.shape[-1]
    return x * c + pltpu.roll(x, w - 8, 1) * s1 + pltpu.roll(x, 8, 1) * s2


def _inproj_kernel(x_ref, shift_ref, scale_ref, w_ref, gkv_ref, gq_ref, wk_ref, wv_ref, wq_ref,
                   pk_ref, vone_ref, rc_ref, rs1_ref, rs2_ref,
                   q_ref, k_ref, vt_ref, s5_ref, mix_ref):
    tm = x_ref.shape[1]
    sub = min(tm, INPROJ_SUB)
    for r0 in range(0, tm, sub):
        rows = slice(r0, r0 + sub)
        x = x_ref[0, rows, :]
        h = _ln_rows(x) * (1.0 + scale_ref[0]) + shift_ref[0]
        proj = _dot(h.astype(BF16), w_ref[...])
        ckv = proj[:, C_CKV:C_CKV + KV_LORA]
        cq = proj[:, C_CQ:C_CQ + Q_LORA]
        kr = proj[:, C_KR:C_KR + LANE]

        rk = ckv * lax.rsqrt(jnp.mean(ckv * ckv, axis=-1, keepdims=True) + LN_EPS) * gkv_ref[...]
        rk = rk.astype(BF16)
        rq = cq * lax.rsqrt(jnp.mean(cq * cq, axis=-1, keepdims=True) + LN_EPS) * gq_ref[...]
        rq = rq.astype(BF16)

        rc, rs1, rs2 = rc_ref[rows, :], rs1_ref[rows, :], rs2_ref[rows, :]
        krp = _rope_lanes(_dot(kr.astype(BF16), pk_ref[...]), rc, rs1, rs2)
        kn = _dot(rk, wk_ref[...])
        vv = _dot(rk, wv_ref[...]) + vone_ref[...]
        qq = _dot(rq, wq_ref[...])
        for hd in range(N_HEADS):
            sl = slice(hd * HEAD_PAD, (hd + 1) * HEAD_PAD)
            q_ref[0, hd, rows, :] = _rope_lanes(qq[:, sl], rc, rs1, rs2).astype(BF16)
            k_ref[0, hd, rows, :] = (kn[:, sl] + krp).astype(BF16)
            vt_ref[0, hd, 0, :, rows] = vv[:, sl].T[:V_ROWS].astype(BF16)
        s5_ref[0, rows, :] = proj[:, C_S5:C_S5 + BRANCH].astype(BF16)
        mix_ref[0, rows, :] = proj[:, C_MIX:C_MIX + MIX_W].astype(BF16)


def _inproj(x, shift, scale, batch_mod, wts, tables, tm):
    B, n, D = x.shape
    w_in, gkv, gq, wk, wv, wq, pk, vone = wts
    rc, rs1, rs2 = tables
    mod_map = (lambda b, i: (b, 0, 0)) if batch_mod else (lambda b, i: (0, 0, 0))
    full2 = lambda b, i: (0, 0)
    tab = pl.BlockSpec((tm, LANE), lambda b, i: (i, 0))
    hw = N_HEADS * HEAD_PAD
    return pl.pallas_call(
        _inproj_kernel,
        out_shape=(jax.ShapeDtypeStruct((B, N_HEADS, n, HEAD_PAD), BF16),
                   jax.ShapeDtypeStruct((B, N_HEADS, n, HEAD_PAD), BF16),
                   jax.ShapeDtypeStruct((B, N_HEADS, n // tm, V_ROWS, tm), BF16),
                   jax.ShapeDtypeStruct((B, n, BRANCH), BF16),
                   jax.ShapeDtypeStruct((B, n, MIX_W), BF16)),
        grid=(B, n // tm),
        in_specs=[pl.BlockSpec((1, tm, D), lambda b, i: (b, i, 0)),
                  pl.BlockSpec((1, 1, D), mod_map),
                  pl.BlockSpec((1, 1, D), mod_map),
                  pl.BlockSpec((D, IN_PAD), full2),
                  pl.BlockSpec((1, KV_LORA), full2),
                  pl.BlockSpec((1, Q_LORA), full2),
                  pl.BlockSpec((KV_LORA, hw), full2),
                  pl.BlockSpec((KV_LORA, hw), full2),
                  pl.BlockSpec((Q_LORA, hw), full2),
                  pl.BlockSpec((LANE, LANE), full2),
                  pl.BlockSpec((1, hw), full2),
                  tab, tab, tab],
        out_specs=(pl.BlockSpec((1, N_HEADS, tm, HEAD_PAD), lambda b, i: (b, 0, i, 0)),
                   pl.BlockSpec((1, N_HEADS, tm, HEAD_PAD), lambda b, i: (b, 0, i, 0)),
                   pl.BlockSpec((1, N_HEADS, 1, V_ROWS, tm), lambda b, i: (b, 0, i, 0, 0)),
                   pl.BlockSpec((1, tm, BRANCH), lambda b, i: (b, i, 0)),
                   pl.BlockSpec((1, tm, MIX_W), lambda b, i: (b, i, 0))),
        compiler_params=pltpu.CompilerParams(
            dimension_semantics=("parallel", "parallel"), vmem_limit_bytes=V7X_VMEM_LIMIT),
        name="inproj",
    )(x, shift, scale, w_in, gkv, gq, wk, wv, wq, pk, vone, rc, rs1, rs2)


ATT_TK = 512


def _attn_kernel(*refs, n_src):
    q_ref = refs[0]
    k_refs = refs[1:1 + n_src]
    vt_refs = refs[1 + n_src:1 + 2 * n_src]
    o_ref = refs[1 + 2 * n_src]
    s_scr, pa_scr, pb_scr, acc_scr, m_scr = refs[2 + 2 * n_src:]
    tq = q_ref.shape[2]
    step = pl.program_id(1)

    @pl.when(step == 0)
    def _():
        s_scr[...] = jnp.zeros(s_scr.shape, F32)
        m_scr[...] = jnp.zeros(m_scr.shape, F32)

    q = q_ref[0, 0]
    m_prev = jnp.max(m_scr[...], axis=0, keepdims=True)
    acc_scr[...] = jnp.zeros(acc_scr.shape, F32)
    m_acc = jnp.full((8, tq), -jnp.inf, F32)

    tiles = []
    row = 0
    for si in range(n_src):
        nt, tk = vt_refs[si].shape[2], vt_refs[si].shape[4]
        for j in range(nt):
            tiles.append((si, j, row, tk))
            row += tk
    p_bufs = ((pa_scr, 0), (pa_scr, 1), (pb_scr, 0), (pb_scr, 1))

    def ex(i):
        si, j, r, tk = tiles[i]
        buf, slot = p_bufs[i % len(p_bufs)]
        buf[slot, 0:tk, :] = jnp.exp2((s_scr[r:r + tk, :] - m_prev).astype(BF16))

    def qk(i, m):
        si, j, r, tk = tiles[i]
        s = _dot_nt(k_refs[si][0, 0, j * tk:(j + 1) * tk, :], q)
        s_scr[r:r + tk, :] = s
        parts = [s[a:a + 8, :] for a in range(0, tk, 8)]
        while len(parts) > 1:
            parts = [jnp.maximum(parts[a], parts[a + 1]) for a in range(0, len(parts), 2)]
        return jnp.maximum(m, parts[0])

    def mm(i):
        si, j, r, tk = tiles[i]
        buf, slot = p_bufs[i % len(p_bufs)]
        acc_scr[...] += _dot(vt_refs[si][0, 0, j], buf[slot, 0:tk, :])

    ex(0)
    for i in range(len(tiles)):
        if i + 1 < len(tiles):
            ex(i + 1)
        m_acc = qk(i, m_acc)
        mm(i)

    m_scr[...] = m_acc

    @pl.when(step > 0)
    def _():
        acc = acc_scr[...]
        o_ref[0, 0] = (acc[:V_DIM, :] / acc[V_DIM:V_DIM + 1, :]).astype(o_ref.dtype)


def _attention(q, ks, vts, tq):
    B, H, n, _ = q.shape
    n_src = len(ks)
    nkeys = sum(k.shape[2] for k in ks)
    tk_max = max(v.shape[4] for v in vts)
    for k, v in zip(ks, vts):
        nt = v.shape[2]
        assert k.shape[2] == nt * v.shape[4]
    nq = n // tq
    n_items = H * nq
    cur = lambda k: jnp.minimum(k, n_items - 1)
    prv = lambda k: jnp.maximum(k - 1, 0)
    k_spec = lambda a: pl.BlockSpec((1, 1) + a.shape[2:], lambda b, k: (b, cur(k) // nq, 0, 0))
    vt_spec = lambda a: pl.BlockSpec((1, 1) + a.shape[2:], lambda b, k: (b, prv(k) // nq, 0, 0, 0))
    return pl.pallas_call(
        functools.partial(_attn_kernel, n_src=n_src),
        out_shape=jax.ShapeDtypeStruct((B, H, V_DIM, n), BF16),
        grid=(B, n_items + 1),
        in_specs=[pl.BlockSpec((1, 1, tq, HEAD_PAD), lambda b, k: (b, cur(k) // nq, cur(k) % nq, 0))]
                 + [k_spec(k) for k in ks] + [vt_spec(v) for v in vts],
        out_specs=pl.BlockSpec((1, 1, V_DIM, tq), lambda b, k: (b, prv(k) // nq, 0, prv(k) % nq)),
        scratch_shapes=[pltpu.VMEM((nkeys, tq), F32),
                        pltpu.VMEM((2, tk_max, tq), BF16), pltpu.VMEM((2, tk_max, tq), BF16),
                        pltpu.VMEM((V_ROWS, tq), F32), pltpu.VMEM((8, tq), F32)],
        compiler_params=pltpu.CompilerParams(
            dimension_semantics=("parallel", "arbitrary"), vmem_limit_bytes=V7X_VMEM_LIMIT),
        name="attention",
    )(q, *ks, *vts)


def _s5_kernel(ul_ref, uc_ref, m_ref, pb_ref, pc_ref, adec_ref, yl_ref, yc_ref,
               ut_ref, sfr_ref, sbr_ref, sfi_ref, sbi_ref, f_ref):
    n = ul_ref.shape[1]
    nc = uc_ref.shape[1]
    kl = n // S5_T
    kc = nc // S5_T
    nblk = n // S5_BLK
    rows_blk = S5_BLK // S5_T
    ncol = ut_ref.shape[2]

    half_g = S5_G // 2

    def relayout_in(nrows, col0):
        for s in range(S5_T):
            for h in range(2):
                blk = f_ref[h, pl.ds(s, nrows, stride=S5_T), :]
                if nrows < LANE:
                    blk = jnp.concatenate([blk, jnp.zeros((LANE - nrows, LANE), F32)], axis=0)
                bt = blk.T
                for gg in range(half_g):
                    ut_ref[h * half_g + gg, s * S5_H:(s + 1) * S5_H, col0:col0 + LANE] = (
                        bt[gg * S5_H:(gg + 1) * S5_H, :].astype(BF16))

    for jb in range(nblk):
        for h in range(2):
            f_ref[h] = ul_ref[0, jb * S5_BLK:(jb + 1) * S5_BLK, h * LANE:(h + 1) * LANE].astype(F32)
        relayout_in(rows_blk, jb * LANE)
    for h in range(2):
        f_ref[h, 0:nc, :] = uc_ref[0, :, h * LANE:(h + 1) * LANE].astype(F32)
    relayout_in(kc, kl)

    comp_refs = (sfr_ref, sbr_ref, sfi_ref, sbi_ref)
    for pair in range(S5_G // 2):
        st0 = _dot(pb_ref[2 * pair], ut_ref[2 * pair])
        st1 = _dot(pb_ref[2 * pair + 1], ut_ref[2 * pair + 1])
        for c, ref in enumerate(comp_refs):
            both = jnp.concatenate([st0[c * S5_P:(c + 1) * S5_P], st1[c * S5_P:(c + 1) * S5_P]], axis=0)
            ref[:, pair * LANE:(pair + 1) * LANE] = both.T

    afr, afi, abr, abi = (adec_ref[c:c + 1, :] for c in range(4))

    def step(rf, rb, carry):
        fr, fi, br, bi = carry
        s_fr, s_fi = sfr_ref[pl.ds(rf, 1), :], sfi_ref[pl.ds(rf, 1), :]
        s_br, s_bi = sbr_ref[pl.ds(rb, 1), :], sbi_ref[pl.ds(rb, 1), :]
        sfr_ref[pl.ds(rf, 1), :] = fr
        sfi_ref[pl.ds(rf, 1), :] = fi
        sbr_ref[pl.ds(rb, 1), :] = br
        sbi_ref[pl.ds(rb, 1), :] = bi
        return (afr * fr - afi * fi + s_fr, afr * fi + afi * fr + s_fi,
                abr * br - abi * bi + s_br, abr * bi + abi * br + s_bi)

    zero = jnp.zeros((1, S5_G * S5_P), F32)
    carry = lax.fori_loop(0, kc, lambda i, c: step(kl + i, kl + kc - 1 - i, c), (zero,) * 4)
    lax.fori_loop(0, kl, lambda i, c: step(i, kl - 1 - i, c), carry)

    for pair in range(S5_G // 2):
        comp_t = [ref[:, pair * LANE:(pair + 1) * LANE].T for ref in comp_refs]
        for k in range(2):
            g = 2 * pair + k
            hin_t = jnp.concatenate([t[k * S5_P:(k + 1) * S5_P] for t in comp_t], axis=0).astype(BF16)
            y = _dot(m_ref[g], ut_ref[g]) + _dot(pc_ref[g], hin_t)
            ut_ref[g] = y.astype(BF16)

    def relayout_out(nrows, col0):
        for t in range(S5_T):
            for h in range(2):
                yt = jnp.concatenate(
                    [ut_ref[h * half_g + gg, t * S5_H:(t + 1) * S5_H, col0:col0 + LANE]
                     for gg in range(half_g)], axis=0).astype(F32)
                f_ref[h, pl.ds(t, nrows, stride=S5_T), :] = yt.T[:nrows]

    for jb in range(nblk):
        relayout_out(rows_blk, jb * LANE)
        for h in range(2):
            yl_ref[0, jb * S5_BLK:(jb + 1) * S5_BLK, h * LANE:(h + 1) * LANE] = (
                f_ref[h].astype(yl_ref.dtype))
    relayout_out(kc, kl)
    for h in range(2):
        yc_ref[0, :, h * LANE:(h + 1) * LANE] = f_ref[h, 0:nc, :].astype(yc_ref.dtype)


def _s5(ul, uc, ops):
    B, n, _ = ul.shape
    nc = uc.shape[1]
    m, pb, pc, adec = ops
    assert n % S5_BLK == 0 and nc % S5_T == 0 and nc // S5_T <= LANE and nc <= S5_BLK
    assert (n // S5_T) % 2 == 0 and (nc // S5_T) % 2 == 0
    ncol = n // S5_T + LANE
    w3 = lambda b: (0, 0, 0)
    return pl.pallas_call(
        _s5_kernel,
        out_shape=(jax.ShapeDtypeStruct((B, n, BRANCH), BF16),
                   jax.ShapeDtypeStruct((B, nc, BRANCH), BF16)),
        grid=(B,),
        in_specs=[pl.BlockSpec((1, n, BRANCH), lambda b: (b, 0, 0)),
                  pl.BlockSpec((1, nc, BRANCH), lambda b: (b, 0, 0)),
                  pl.BlockSpec(m.shape, w3), pl.BlockSpec(pb.shape, w3), pl.BlockSpec(pc.shape, w3),
                  pl.BlockSpec(adec.shape, lambda b: (0, 0))],
        out_specs=(pl.BlockSpec((1, n, BRANCH), lambda b: (b, 0, 0)),
                   pl.BlockSpec((1, nc, BRANCH), lambda b: (b, 0, 0))),
        scratch_shapes=[pltpu.VMEM((S5_G, S5_T * S5_H, ncol), BF16),
                        pltpu.VMEM((ncol, S5_G * S5_P), F32), pltpu.VMEM((ncol, S5_G * S5_P), F32),
                        pltpu.VMEM((ncol, S5_G * S5_P), F32), pltpu.VMEM((ncol, S5_G * S5_P), F32),
                        pltpu.VMEM((2, S5_BLK, LANE), F32)],
        compiler_params=pltpu.CompilerParams(
            dimension_semantics=("parallel",), vmem_limit_bytes=V7X_VMEM_LIMIT),
        name="s5",
    )(ul, uc, m, pb, pc, adec)


POOL_HALO = 16


def _gelu_tanh(x):
    return 0.5 * x * (1.0 + jnp.tanh(math.sqrt(2.0 / math.pi) * (x + 0.044715 * (x * x * x))))


def _merge_kernel(x_ref, att_ref, y5_ref, mix_ref, prev_ref, next_ref, gate_ref,
                  wpool_ref, pscale_ref, wglu_ref, bglu_ref, sg_ref, sb_ref, ws_ref, bs_ref,
                  wout_ref, lng_ref, lnb_ref, o_ref, a_ref, b2_ref, b4_ref, b8_ref,
                  *, n_total, alpha):
    tm = x_ref.shape[1]
    i = pl.program_id(1)
    ni = pl.num_programs(1)
    hl = POOL_HALO

    pool_in = mix_ref[0, :, 0:BRANCH].astype(F32)
    a_ref[0:hl, :] = jnp.where(i > 0, prev_ref[0].astype(F32), 0.0)
    a_ref[hl:hl + tm, :] = pool_in
    a_ref[hl + tm:hl + tm + hl, :] = jnp.where(i < ni - 1, next_ref[0].astype(F32), 0.0)
    r2 = tm + 16
    b2_ref[8:8 + r2, :] = a_ref[7:7 + r2, :] + a_ref[8:8 + r2, :]
    r4 = tm + 12
    b4_ref[10:10 + r4, :] = b2_ref[9:9 + r4, :] + b2_ref[11:11 + r4, :]
    r8 = tm + 8
    b8_ref[12:12 + r8, :] = b4_ref[10:10 + r8, :] + b4_ref[14:14 + r8, :]
    lane = lax.broadcasted_iota(jnp.int32, (CHUNK, BRANCH), 1)
    grp = lane // POOL_GC
    half = jnp.left_shift(1, grp)
    lane_c = lane // SGU_HD

    for c in range(tm // CHUNK):
        r0 = c * CHUNK
        rows = slice(r0, r0 + CHUNK)
        c2 = b2_ref[hl + r0:hl + r0 + CHUNK, :]
        c4 = b4_ref[hl + r0:hl + r0 + CHUNK, :]
        c8 = b8_ref[hl + r0:hl + r0 + CHUNK, :]
        c16 = b8_ref[hl - 4 + r0:hl - 4 + r0 + CHUNK, :] + b8_ref[hl + 4 + r0:hl + 4 + r0 + CHUNK, :]
        tot = jnp.where(grp == 0, c2, jnp.where(grp == 1, c4, jnp.where(grp == 2, c8, c16)))
        t = lax.broadcasted_iota(jnp.int32, (CHUNK, BRANCH), 0) + (i * tm + r0)
        cnt = jnp.minimum(t + half, n_total) - jnp.maximum(t - half, 0)
        pooled = tot / cnt.astype(F32) - a_ref[hl + r0:hl + r0 + CHUNK, :]
        pool_l = _dot(pooled.astype(BF16), wpool_ref[...]) * pscale_ref[...]

        g1 = _gelu_tanh(y5_ref[0, rows, :].astype(F32))
        s5_l = g1 * _sigmoid(_dot(g1.astype(BF16), wglu_ref[...]) + bglu_ref[...])

        su = mix_ref[0, rows, BRANCH:2 * BRANCH].astype(F32)
        sv = mix_ref[0, rows, 2 * BRANCH:3 * BRANCH].astype(F32)
        vc = (_ln_rows(sv) * sg_ref[...] + sb_ref[...]).astype(BF16)
        r = _dot(ws_ref[SGU_HEADS - 1], vc)
        for hd in range(SGU_HEADS - 2, -1, -1):
            r = jnp.where(lane_c == hd, _dot(ws_ref[hd], vc), r)
        sgu_l = su * (r + bs_ref[...])

        gts = mix_ref[0, rows, 3 * BRANCH:].astype(F32)
        att = att_ref[0, :, :, rows].astype(F32).reshape(N_HEADS * V_DIM, CHUNK).T
        cat = jnp.concatenate([att, pool_l, s5_l, sgu_l], axis=-1)
        y = _dot((cat * _silu(gts)).astype(BF16), wout_ref[...])
        z = alpha * x_ref[0, rows, :] + gate_ref[0] * y
        o_ref[0, rows, :] = _ln_rows(z) * lng_ref[...] + lnb_ref[...]


def _merge(x, att, y5, mix, gate, batch_mod, wts, tm, alpha):
    B, n, D = x.shape
    wpool, pscale, wglu, bglu, sg, sb, ws, bs, wout, lng, lnb = wts
    hb = tm // POOL_HALO
    nhb = n // POOL_HALO
    full2 = lambda b, i: (0, 0)
    mod_map = (lambda b, i: (b, 0, 0)) if batch_mod else (lambda b, i: (0, 0, 0))
    row = lambda w: pl.BlockSpec((1, w), full2)
    return pl.pallas_call(
        functools.partial(_merge_kernel, n_total=n, alpha=alpha),
        out_shape=jax.ShapeDtypeStruct((B, n, D), F32),
        grid=(B, n // tm),
        in_specs=[pl.BlockSpec((1, tm, D), lambda b, i: (b, i, 0)),
                  pl.BlockSpec((1, N_HEADS, V_DIM, tm), lambda b, i: (b, 0, 0, i)),
                  pl.BlockSpec((1, tm, BRANCH), lambda b, i: (b, i, 0)),
                  pl.BlockSpec((1, tm, MIX_W), lambda b, i: (b, i, 0)),
                  pl.BlockSpec((1, POOL_HALO, BRANCH),
                               lambda b, i: (b, jnp.maximum(i * hb - 1, 0), 0)),
                  pl.BlockSpec((1, POOL_HALO, BRANCH),
                               lambda b, i: (b, jnp.minimum((i + 1) * hb, nhb - 1), 0)),
                  pl.BlockSpec((1, 1, D), mod_map),
                  pl.BlockSpec((BRANCH, BRANCH), full2), row(BRANCH),
                  pl.BlockSpec((BRANCH, BRANCH), full2), row(BRANCH),
                  row(BRANCH), row(BRANCH),
                  pl.BlockSpec((SGU_HEADS, CHUNK, CHUNK), lambda b, i: (0, 0, 0)),
                  pl.BlockSpec((CHUNK, BRANCH), full2),
                  pl.BlockSpec((4 * BRANCH, D), full2), row(D), row(D)],
        out_specs=pl.BlockSpec((1, tm, D), lambda b, i: (b, i, 0)),
        scratch_shapes=[pltpu.VMEM((tm + 2 * POOL_HALO, BRANCH), F32) for _ in range(4)],
        compiler_params=pltpu.CompilerParams(
            dimension_semantics=("parallel", "parallel"), vmem_limit_bytes=V7X_VMEM_LIMIT),
        name="merge",
    )(x, att, y5, mix, mix, mix, gate, wpool, pscale, wglu, bglu, sg, sb, ws, bs, wout, lng, lnb)


def _prep_inproj_weights(w_in, g_q, w_uq, g_kv, w_ukv):
    D = w_in.shape[0]
    o_kr = KV_LORA
    o_s5 = o_kr + ROPE
    o_cq = o_s5 + BRANCH
    o_rest = o_cq + Q_LORA
    wb = w_in.astype(BF16)
    zpad = lambda k: jnp.zeros((D, k), BF16)
    w = jnp.concatenate(
        [wb[:, 0:KV_LORA], wb[:, o_cq:o_cq + Q_LORA], zpad(C_KR - C_CQ - Q_LORA),
         wb[:, o_kr:o_kr + ROPE], zpad(C_S5 - C_KR - ROPE), wb[:, o_s5:o_s5 + BRANCH],
         wb[:, o_rest:]], axis=1)
    assert w.shape == (D, IN_PAD)
    hw = N_HEADS * HEAD_PAD
    scale = (NOPE + ROPE) ** -0.5 * math.log2(math.e)
    ukv = w_ukv.reshape(KV_LORA, N_HEADS, NOPE + V_DIM)
    uq = w_uq.reshape(Q_LORA, N_HEADS, NOPE + ROPE) * scale
    padl = lambda a: jnp.pad(a, ((0, 0), (0, 0), (0, HEAD_PAD - a.shape[2])))
    wk = padl(ukv[:, :, :NOPE])
    wv = padl(ukv[:, :, NOPE:])
    wq = padl(uq)
    pk = (jnp.arange(LANE)[:, None] + NOPE == jnp.arange(LANE)[None, :]) & (jnp.arange(LANE)[:, None] < ROPE)
    vone = jnp.tile(jnp.arange(HEAD_PAD) == V_DIM, N_HEADS).astype(F32)
    return (w, g_kv.reshape(1, KV_LORA), g_q.reshape(1, Q_LORA),
            wk.reshape(KV_LORA, hw).astype(BF16), wv.reshape(KV_LORA, hw).astype(BF16),
            wq.reshape(Q_LORA, hw).astype(BF16), pk.astype(BF16), vone.reshape(1, hw))


def _rope_tables(n, rotate):
    ones = jnp.ones((n, NOPE), F32)
    zpad = jnp.zeros((n, HEAD_PAD - NOPE - ROPE), F32)
    if not rotate:
        c = jnp.concatenate([ones, jnp.ones((n, ROPE), F32), zpad], axis=1)
        z = jnp.zeros((n, HEAD_PAD), F32)
        return c, z, z
    nf = ROPE // 4
    rows = n // GRID_W
    lane = np.arange(HEAD_PAD)
    blk = (lane - NOPE) // (2 * nf)
    in_rope = (lane >= NOPE) & (lane < NOPE + ROPE)
    first = ((lane - NOPE) % (2 * nf)) < nf
    inv = ROPE_BASE ** (-jnp.arange(nf, dtype=F32) / nf)
    inv_lane = jnp.tile(inv, HEAD_PAD // nf)
    is_row = jnp.asarray(in_rope & (blk == 0))
    is_col = jnp.asarray(in_rope & (blk == 1))
    ang_r = jnp.arange(rows, dtype=F32)[:, None] * inv_lane[None, :]
    ang_c = jnp.arange(GRID_W, dtype=F32)[:, None] * inv_lane[None, :]

    def grid(fr, fc, base):
        t3 = (jnp.where(is_row, fr, 0.0)[:, None, :] + jnp.where(is_col, fc, 0.0)[None, :, :]
              + base[None, None, :])
        return t3.reshape(n, HEAD_PAD)

    nope_one = jnp.asarray((lane < NOPE).astype(np.float32))
    zero = jnp.zeros((HEAD_PAD,), F32)
    m1 = jnp.asarray(first)
    c = grid(jnp.cos(ang_r), jnp.cos(ang_c), nope_one)
    s1 = grid(jnp.where(m1, -jnp.sin(ang_r), 0.0), jnp.where(m1, -jnp.sin(ang_c), 0.0), zero)
    s2 = grid(jnp.where(m1, 0.0, jnp.sin(ang_r)), jnp.where(m1, 0.0, jnp.sin(ang_c)), zero)
    return c, s1, s2


def _s5_selectors():
    T, H = S5_T, S5_H
    nslot = 2 * T - 1
    toep = np.zeros((T, nslot * H, T * H), np.float32)
    for t in range(T):
        for s in range(T):
            slot = 0 if t == s else (t - s if t > s else T - 1 + s - t)
            for h in range(H):
                toep[t, slot * H + h, s * H + h] = 1.0
    e_s = np.kron(np.eye(T, dtype=np.float32), np.ones((1, H), np.float32))
    e_h = np.kron(np.ones((1, T), np.float32), np.eye(H, dtype=np.float32))
    return toep, e_s, e_h


def _prep_s5(lam_re, lam_im, log_dt, b_re, b_im, c_re, c_im, s5_d):
    T, G, P, H = S5_T, S5_G, S5_P, S5_H
    hi = lax.Precision.HIGHEST
    toep, e_s, e_h = _s5_selectors()
    lam_re = lam_re.astype(F32)
    lam_im = lam_im.astype(F32)
    dt = jnp.exp(log_dt.astype(F32))[..., None]
    zr, zi = lam_re * dt, lam_im * dt
    j = jnp.arange(T + 1, dtype=F32)[None, None, :, None]
    mag = jnp.exp(zr[:, :, None, :] * j)
    ang = zi[:, :, None, :] * j
    pr, pi = mag * jnp.cos(ang), mag * jnp.sin(ang)
    nr, ni = pr[:, :, 1] - 1.0, pi[:, :, 1]
    den = lam_re * lam_re + lam_im * lam_im
    k_re = (nr * lam_re + ni * lam_im) / den
    k_im = (ni * lam_re - nr * lam_im) / den
    bb_re = k_re[..., None] * b_re - k_im[..., None] * b_im
    bb_im = k_re[..., None] * b_im + k_im[..., None] * b_re
    cre, cim = c_re[:, :, None, :, :], c_im[:, :, None, :, :]
    prb, pib = pr[:, :, :, None, :], pi[:, :, :, None, :]
    cl_re = cre * prb - cim * pib
    cl_im = cre * pib + cim * prb
    kk = (jnp.einsum('dgjop,dgph->dgjoh', cl_re[:, :, :T], bb_re, precision=hi)
          - jnp.einsum('dgjop,dgph->dgjoh', cl_im[:, :, :T], bb_im, precision=hi))
    skip = jnp.eye(H, dtype=F32)[None] * s5_d.astype(F32).reshape(G, H, 1)
    x = jnp.concatenate([(kk[0, :, 0] + kk[1, :, 0] + skip)[:, None], kk[0, :, 1:], kk[1, :, 1:]],
                        axis=1)
    x = x.transpose(0, 2, 1, 3).reshape(G, H, (2 * T - 1) * H)
    m = jnp.einsum('goJ,tJc->gtoc', x.astype(BF16), jnp.asarray(toep, BF16),
                   preferred_element_type=F32).reshape(G, T * H, T * H)
    over_s = lambda a: jnp.einsum('gap,ac->gpc', a, jnp.asarray(e_s), precision=hi)
    over_h = lambda a: jnp.einsum('gpa,ac->gpc', a, jnp.asarray(e_h), precision=hi)

    def lb(p_r, p_i, d):
        ar, ai, br, bi = over_s(p_r), over_s(p_i), over_h(bb_re[d]), over_h(bb_im[d])
        return ar * br - ai * bi, ar * bi + ai * br

    f_re, f_im = lb(pr[0, :, T - 1::-1], pi[0, :, T - 1::-1], 0)
    b_re_, b_im_ = lb(pr[1, :, :T], pi[1, :, :T], 1)
    pb = jnp.concatenate([f_re, b_re_, f_im, b_im_], axis=1)
    to_c = lambda a: a.reshape(G, T * H, P)
    zf_re, zf_im = cl_re[0, :, 1:], cl_im[0, :, 1:]
    zb_re, zb_im = cl_re[1, :, T:0:-1], cl_im[1, :, T:0:-1]
    pc = jnp.concatenate([to_c(zf_re), to_c(zb_re), -to_c(zf_im), -to_c(zb_im)], axis=2)
    adec = jnp.stack([pr[0, :, T], pi[0, :, T], pr[1, :, T], pi[1, :, T]], axis=0).reshape(4, G * P)
    return (m.astype(BF16), pb.astype(BF16), pc.astype(BF16), adec)


def _prep_merge_weights(w_pool, pool_scale, w_glu, b_glu, sgu_g, sgu_b, w_s, b_s, w_out, ln_g, ln_b):
    z = jnp.zeros((POOL_GC, POOL_GC), F32)
    ng = len(POOL_WINDOWS)
    wpool = jnp.concatenate(
        [jnp.concatenate([w_pool[gi] if gj == gi else z for gj in range(ng)], axis=1)
         for gi in range(ng)], axis=0)
    bs = jnp.repeat(b_s.T, SGU_HD, axis=1)
    r = lambda a: a.reshape(1, -1).astype(F32)
    return (wpool.astype(BF16), r(pool_scale), w_glu.astype(BF16), r(b_glu), r(sgu_g), r(sgu_b),
            w_s.astype(BF16), bs.astype(F32), w_out.astype(BF16), r(ln_g), r(ln_b))


def kernel(x, c, ctx, c_ctx, w_mod, b_mod, w_in, g_q, w_uq, g_kv, w_ukv, w_pool, pool_scale, lam_re, lam_im, log_dt, s5_b_re, s5_b_im, s5_c_re, s5_c_im, s5_d, w_glu, b_glu, sgu_g, sgu_b, w_s, b_s, w_out, ln_g, ln_b):
    B, n, D = x.shape
    nc = ctx.shape[1]
    depth = w_mod.shape[0]
    alpha = (2 * depth) ** 0.25
    assert B + 1 <= 8
    tm = min(ATT_TK, n)
    tmc = min(ATT_TK, nc)
    tq = min(512, n)
    tqc = min(512, nc)

    cc = jnp.concatenate([c.astype(F32), c_ctx.astype(F32).reshape(1, D),
                          jnp.zeros((8 - B - 1, D), F32)], axis=0)
    mod = _modulation(cc, w_mod, b_mod.reshape(depth, 1, 3 * D))
    tab_l = _rope_tables(n, True)
    tab_c = _rope_tables(nc, False)

    wts_in_all = jax.vmap(_prep_inproj_weights)(w_in, g_q, w_uq, g_kv, w_ukv)
    s5_ops_all = jax.vmap(_prep_s5)(lam_re, lam_im, log_dt, s5_b_re, s5_b_im, s5_c_re, s5_c_im, s5_d)
    wts_mg_all = jax.vmap(_prep_merge_weights)(w_pool, pool_scale, w_glu, b_glu, sgu_g, sgu_b,
                                               w_s, b_s, w_out, ln_g, ln_b)

    for l in range(depth):
        last = l == depth - 1
        shift = mod[l, :, 0:D].reshape(8, 1, D)
        scale = mod[l, :, D:2 * D].reshape(8, 1, D)
        gate = mod[l, :, 2 * D:].reshape(8, 1, D)
        wts_in = tuple(a[l] for a in wts_in_all)
        s5_ops = tuple(a[l] for a in s5_ops_all)
        wts_mg = tuple(a[l] for a in wts_mg_all)

        q_l, k_l, v_l, s5u_l, mix_l = _inproj(x, shift[:B], scale[:B], True, wts_in, tab_l, tm)
        q_c, k_c, v_c, s5u_c, mix_c = _inproj(ctx, shift[B:B + 1], scale[B:B + 1], False, wts_in, tab_c, tmc)
        att_l = _attention(q_l, [k_l, k_c], [v_l, v_c], tq)
        y5_l, y5_c = _s5(s5u_l, s5u_c, s5_ops)
        x_new = _merge(x, att_l, y5_l, mix_l, gate[:B], True, wts_mg, tm, alpha)
        if not last:
            att_c = _attention(q_c, [k_c], [v_c], tqc)
            ctx = _merge(ctx, att_c, y5_c, mix_c, gate[B:B + 1], False, wts_mg, tmc, alpha)
        x = x_new
    return x
```

```python
import functools
import math

import jax
import jax.numpy as jnp
import numpy as np
from jax import lax
from jax.experimental import pallas as pl
from jax.experimental.pallas import tpu as pltpu

F32 = jnp.float32
BF16 = jnp.bfloat16

GRID_W = 64
N_HEADS = 4
NOPE = 64
ROPE = 32
V_DIM = 64
Q_LORA = 192
KV_LORA = 128
ROPE_BASE = 10000.0
BRANCH = 256
POOL_WINDOWS = (2, 4, 8, 16)
POOL_GC = 64
S5_H = 16
S5_G = 16
S5_P = 64
SGU_HEADS = 4
SGU_HD = 64
CHUNK = 128
LN_EPS = 1e-6

LANE = 128
HEAD_PAD = 128
V_ROWS = 80
V7X_VMEM_LIMIT = 56 * 1024 * 1024
S5_T = 16
S5_BLK = S5_T * LANE

C_CKV = 0
C_CQ = 128
C_KR = 384
C_S5 = 512
C_MIX = 768
IN_PAD = 2560
INPROJ_SUB = 128
ROW_TILE = 1024
MIX_W = 3 * BRANCH + 4 * BRANCH


def _dot(a, b):
    return jnp.dot(a, b, preferred_element_type=F32)


def _dot_nt(a, b):
    return lax.dot_general(a, b, (((1,), (1,)), ((), ())), preferred_element_type=F32)


def _ln_rows(x):
    mu = jnp.mean(x, axis=-1, keepdims=True)
    xc = x - mu
    var = jnp.mean(xc * xc, axis=-1, keepdims=True)
    return xc * lax.rsqrt(var + LN_EPS)


def _sigmoid(x):
    return 0.5 * jnp.tanh(0.5 * x) + 0.5


def _silu(x):
    h = 0.5 * x
    return h + h * jnp.tanh(h)


def _mod_kernel(c_ref, w_ref, b_ref, o_ref):
    c = c_ref[...]
    s = c * _sigmoid(c)
    o_ref[0] = jnp.dot(s, w_ref[0], preferred_element_type=F32,
                       precision=lax.Precision.HIGHEST) + b_ref[0]


def _modulation(cc, w_mod, b_mod):
    L, D, D3 = w_mod.shape
    nb = D3 // D
    return pl.pallas_call(
        _mod_kernel,
        out_shape=jax.ShapeDtypeStruct((L, 8, D3), F32),
        grid=(L, nb),
        in_specs=[pl.BlockSpec((8, D), lambda l, j: (0, 0)),
                  pl.BlockSpec((1, D, D), lambda l, j: (l, 0, j)),
                  pl.BlockSpec((1, 1, D), lambda l, j: (l, 0, j))],
        out_specs=pl.BlockSpec((1, 8, D), lambda l, j: (l, 0, j)),
        compiler_params=pltpu.CompilerParams(vmem_limit_bytes=V7X_VMEM_LIMIT),
        name="modulation",
    )(cc, w_mod, b_mod)


def _rope_lanes(x, c, s1, s2):
    w = x.shape[-1]
    return x * c + pltpu.roll(x, w - 8, 1) * s1 + pltpu.roll(x, 8, 1) * s2


def _inproj_kernel(x_ref, shift_ref, scale_ref, w_ref, gkv_ref, gq_ref, wk_ref, wv_ref, wq_ref,
                   pk_ref, vone_ref, rc_ref, rs1_ref, rs2_ref,
                   q_ref, k_ref, vt_ref, s5_ref, mix_ref):
    tm = x_ref.shape[1]
    sub = min(tm, INPROJ_SUB)
    for r0 in range(0, tm, sub):
        rows = slice(r0, r0 + sub)
        x = x_ref[0, rows, :]
        h = _ln_rows(x) * (1.0 + scale_ref[0]) + shift_ref[0]
        proj = _dot(h.astype(BF16), w_ref[...])
        ckv = proj[:, C_CKV:C_CKV + KV_LORA]
        cq = proj[:, C_CQ:C_CQ + Q_LORA]
        kr = proj[:, C_KR:C_KR + LANE]

        rk = ckv * lax.rsqrt(jnp.mean(ckv * ckv, axis=-1, keepdims=True) + LN_EPS) * gkv_ref[...]
        rk = rk.astype(BF16)
        rq = cq * lax.rsqrt(jnp.mean(cq * cq, axis=-1, keepdims=True) + LN_EPS) * gq_ref[...]
        rq = rq.astype(BF16)

        rc, rs1, rs2 = rc_ref[rows, :], rs1_ref[rows, :], rs2_ref[rows, :]
        krp = _rope_lanes(_dot(kr.astype(BF16), pk_ref[...]), rc, rs1, rs2)
        kn = _dot(rk, wk_ref[...])
        vv = _dot(rk, wv_ref[...]) + vone_ref[...]
        qq = _dot(rq, wq_ref[...])
        for hd in range(N_HEADS):
            sl = slice(hd * HEAD_PAD, (hd + 1) * HEAD_PAD)
            q_ref[0, hd, rows, :] = _rope_lanes(qq[:, sl], rc, rs1, rs2).astype(BF16)
            k_ref[0, hd, rows, :] = (kn[:, sl] + krp).astype(BF16)
            tkv = vt_ref.shape[4]
            vt_ref[0, hd, r0 // tkv, :, r0 % tkv:r0 % tkv + sub] = vv[:, sl].T[:V_ROWS].astype(BF16)
        s5_ref[0, rows, :] = proj[:, C_S5:C_S5 + BRANCH].astype(BF16)
        mix_ref[0, rows, :] = proj[:, C_MIX:C_MIX + MIX_W].astype(BF16)


def _inproj(x, shift, scale, batch_mod, wts, tables, tm):
    B, n, D = x.shape
    w_in, gkv, gq, wk, wv, wq, pk, vone = wts
    rc, rs1, rs2 = tables
    tk = min(ATT_TK, n)
    assert tm % tk == 0 and tk % min(tm, INPROJ_SUB) == 0
    mod_map = (lambda b, i: (b, 0, 0)) if batch_mod else (lambda b, i: (0, 0, 0))
    full2 = lambda b, i: (0, 0)
    tab = pl.BlockSpec((tm, LANE), lambda b, i: (i, 0))
    hw = N_HEADS * HEAD_PAD
    return pl.pallas_call(
        _inproj_kernel,
        out_shape=(jax.ShapeDtypeStruct((B, N_HEADS, n, HEAD_PAD), BF16),
                   jax.ShapeDtypeStruct((B, N_HEADS, n, HEAD_PAD), BF16),
                   jax.ShapeDtypeStruct((B, N_HEADS, n // tk, V_ROWS, tk), BF16),
                   jax.ShapeDtypeStruct((B, n, BRANCH), BF16),
                   jax.ShapeDtypeStruct((B, n, MIX_W), BF16)),
        grid=(B, n // tm),
        in_specs=[pl.BlockSpec((1, tm, D), lambda b, i: (b, i, 0)),
                  pl.BlockSpec((1, 1, D), mod_map),
                  pl.BlockSpec((1, 1, D), mod_map),
                  pl.BlockSpec((D, IN_PAD), full2),
                  pl.BlockSpec((1, KV_LORA), full2),
                  pl.BlockSpec((1, Q_LORA), full2),
                  pl.BlockSpec((KV_LORA, hw), full2),
                  pl.BlockSpec((KV_LORA, hw), full2),
                  pl.BlockSpec((Q_LORA, hw), full2),
                  pl.BlockSpec((LANE, LANE), full2),
                  pl.BlockSpec((1, hw), full2),
                  tab, tab, tab],
        out_specs=(pl.BlockSpec((1, N_HEADS, tm, HEAD_PAD), lambda b, i: (b, 0, i, 0)),
                   pl.BlockSpec((1, N_HEADS, tm, HEAD_PAD), lambda b, i: (b, 0, i, 0)),
                   pl.BlockSpec((1, N_HEADS, tm // tk, V_ROWS, tk), lambda b, i: (b, 0, i, 0, 0)),
                   pl.BlockSpec((1, tm, BRANCH), lambda b, i: (b, i, 0)),
                   pl.BlockSpec((1, tm, MIX_W), lambda b, i: (b, i, 0))),
        compiler_params=pltpu.CompilerParams(
            dimension_semantics=("parallel", "parallel"), vmem_limit_bytes=V7X_VMEM_LIMIT),
        name="inproj",
    )(x, shift, scale, w_in, gkv, gq, wk, wv, wq, pk, vone, rc, rs1, rs2)


ATT_TK = 512


def _attn_kernel(*refs, n_src):
    q_ref = refs[0]
    k_refs = refs[1:1 + n_src]
    vt_refs = refs[1 + n_src:1 + 2 * n_src]
    o_ref = refs[1 + 2 * n_src]
    s_scr, pa_scr, pb_scr, acc_scr, m_scr = refs[2 + 2 * n_src:]
    tq = q_ref.shape[2]
    step = pl.program_id(1)

    @pl.when(step == 0)
    def _():
        s_scr[...] = jnp.zeros(s_scr.shape, F32)
        m_scr[...] = jnp.zeros(m_scr.shape, F32)

    q = q_ref[0, 0]
    m_prev = jnp.max(m_scr[...], axis=0, keepdims=True)
    acc_scr[...] = jnp.zeros(acc_scr.shape, F32)
    m_acc = jnp.full((8, tq), -jnp.inf, F32)

    tiles = []
    row = 0
    for si in range(n_src):
        nt, tk = vt_refs[si].shape[2], vt_refs[si].shape[4]
        for j in range(nt):
            tiles.append((si, j, row, tk))
            row += tk
    p_bufs = ((pa_scr, 0), (pa_scr, 1), (pb_scr, 0), (pb_scr, 1))

    def ex(i):
        si, j, r, tk = tiles[i]
        buf, slot = p_bufs[i % len(p_bufs)]
        buf[slot, 0:tk, :] = jnp.exp2((s_scr[r:r + tk, :] - m_prev).astype(BF16))

    def qk(i, m):
        si, j, r, tk = tiles[i]
        s = _dot_nt(k_refs[si][0, 0, j * tk:(j + 1) * tk, :], q)
        s_scr[r:r + tk, :] = s
        parts = [s[a:a + 8, :] for a in range(0, tk, 8)]
        while len(parts) > 1:
            parts = [jnp.maximum(parts[a], parts[a + 1]) for a in range(0, len(parts), 2)]
        return jnp.maximum(m, parts[0])

    def mm(i):
        si, j, r, tk = tiles[i]
        buf, slot = p_bufs[i % len(p_bufs)]
        acc_scr[...] += _dot(vt_refs[si][0, 0, j], buf[slot, 0:tk, :])

    ex(0)
    for i in range(len(tiles)):
        if i + 1 < len(tiles):
            ex(i + 1)
        m_acc = qk(i, m_acc)
        mm(i)

    m_scr[...] = m_acc

    @pl.when(step > 0)
    def _():
        acc = acc_scr[...]
        o_ref[0, 0] = (acc[:V_DIM, :] / acc[V_DIM:V_DIM + 1, :]).astype(o_ref.dtype)


def _attention(q, ks, vts, tq):
    B, H, n, _ = q.shape
    n_src = len(ks)
    nkeys = sum(k.shape[2] for k in ks)
    tk_max = max(v.shape[4] for v in vts)
    for k, v in zip(ks, vts):
        nt = v.shape[2]
        assert k.shape[2] == nt * v.shape[4]
    nq = n // tq
    n_items = H * nq
    cur = lambda k: jnp.minimum(k, n_items - 1)
    prv = lambda k: jnp.maximum(k - 1, 0)
    k_spec = lambda a: pl.BlockSpec((1, 1) + a.shape[2:], lambda b, k: (b, cur(k) // nq, 0, 0))
    vt_spec = lambda a: pl.BlockSpec((1, 1) + a.shape[2:], lambda b, k: (b, prv(k) // nq, 0, 0, 0))
    return pl.pallas_call(
        functools.partial(_attn_kernel, n_src=n_src),
        out_shape=jax.ShapeDtypeStruct((B, H, V_DIM, n), BF16),
        grid=(B, n_items + 1),
        in_specs=[pl.BlockSpec((1, 1, tq, HEAD_PAD), lambda b, k: (b, cur(k) // nq, cur(k) % nq, 0))]
                 + [k_spec(k) for k in ks] + [vt_spec(v) for v in vts],
        out_specs=pl.BlockSpec((1, 1, V_DIM, tq), lambda b, k: (b, prv(k) // nq, 0, prv(k) % nq)),
        scratch_shapes=[pltpu.VMEM((nkeys, tq), F32),
                        pltpu.VMEM((2, tk_max, tq), BF16), pltpu.VMEM((2, tk_max, tq), BF16),
                        pltpu.VMEM((V_ROWS, tq), F32), pltpu.VMEM((8, tq), F32)],
        compiler_params=pltpu.CompilerParams(
            dimension_semantics=("parallel", "arbitrary"), vmem_limit_bytes=V7X_VMEM_LIMIT),
        name="attention",
    )(q, *ks, *vts)


def _s5_kernel(ul_ref, uc_ref, m_ref, pb_ref, pc_ref, adec_ref, yl_ref, yc_ref,
               ut_ref, sfr_ref, sbr_ref, sfi_ref, sbi_ref, f_ref):
    n = ul_ref.shape[1]
    nc = uc_ref.shape[1]
    kl = n // S5_T
    kc = nc // S5_T
    nblk = n // S5_BLK
    rows_blk = S5_BLK // S5_T
    ncol = ut_ref.shape[2]

    half_g = S5_G // 2

    def relayout_in(nrows, col0):
        for s in range(S5_T):
            for h in range(2):
                blk = f_ref[h, pl.ds(s, nrows, stride=S5_T), :]
                if nrows < LANE:
                    blk = jnp.concatenate([blk, jnp.zeros((LANE - nrows, LANE), F32)], axis=0)
                bt = blk.T
                for gg in range(half_g):
                    ut_ref[h * half_g + gg, s * S5_H:(s + 1) * S5_H, col0:col0 + LANE] = (
                        bt[gg * S5_H:(gg + 1) * S5_H, :].astype(BF16))

    for jb in range(nblk):
        for h in range(2):
            f_ref[h] = ul_ref[0, jb * S5_BLK:(jb + 1) * S5_BLK, h * LANE:(h + 1) * LANE].astype(F32)
        relayout_in(rows_blk, jb * LANE)
    for h in range(2):
        f_ref[h, 0:nc, :] = uc_ref[0, :, h * LANE:(h + 1) * LANE].astype(F32)
    relayout_in(kc, kl)

    comp_refs = (sfr_ref, sbr_ref, sfi_ref, sbi_ref)
    for pair in range(S5_G // 2):
        st0 = _dot(pb_ref[2 * pair], ut_ref[2 * pair])
        st1 = _dot(pb_ref[2 * pair + 1], ut_ref[2 * pair + 1])
        for c, ref in enumerate(comp_refs):
            both = jnp.concatenate([st0[c * S5_P:(c + 1) * S5_P], st1[c * S5_P:(c + 1) * S5_P]], axis=0)
            ref[:, pair * LANE:(pair + 1) * LANE] = both.T

    afr, afi, abr, abi = (adec_ref[c:c + 1, :] for c in range(4))

    def step(rf, rb, carry):
        fr, fi, br, bi = carry
        s_fr, s_fi = sfr_ref[pl.ds(rf, 1), :], sfi_ref[pl.ds(rf, 1), :]
        s_br, s_bi = sbr_ref[pl.ds(rb, 1), :], sbi_ref[pl.ds(rb, 1), :]
        sfr_ref[pl.ds(rf, 1), :] = fr
        sfi_ref[pl.ds(rf, 1), :] = fi
        sbr_ref[pl.ds(rb, 1), :] = br
        sbi_ref[pl.ds(rb, 1), :] = bi
        return (afr * fr - afi * fi + s_fr, afr * fi + afi * fr + s_fi,
                abr * br - abi * bi + s_br, abr * bi + abi * br + s_bi)

    zero = jnp.zeros((1, S5_G * S5_P), F32)
    carry = lax.fori_loop(0, kc, lambda i, c: step(kl + i, kl + kc - 1 - i, c), (zero,) * 4)
    lax.fori_loop(0, kl, lambda i, c: step(i, kl - 1 - i, c), carry)

    for pair in range(S5_G // 2):
        comp_t = [ref[:, pair * LANE:(pair + 1) * LANE].T for ref in comp_refs]
        for k in range(2):
            g = 2 * pair + k
            hin_t = jnp.concatenate([t[k * S5_P:(k + 1) * S5_P] for t in comp_t], axis=0).astype(BF16)
            y = _dot(m_ref[g], ut_ref[g]) + _dot(pc_ref[g], hin_t)
            ut_ref[g] = y.astype(BF16)

    def relayout_out(nrows, col0):
        for t in range(S5_T):
            for h in range(2):
                yt = jnp.concatenate(
                    [ut_ref[h * half_g + gg, t * S5_H:(t + 1) * S5_H, col0:col0 + LANE]
                     for gg in range(half_g)], axis=0).astype(F32)
                f_ref[h, pl.ds(t, nrows, stride=S5_T), :] = yt.T[:nrows]

    for jb in range(nblk):
        relayout_out(rows_blk, jb * LANE)
        for h in range(2):
            yl_ref[0, jb * S5_BLK:(jb + 1) * S5_BLK, h * LANE:(h + 1) * LANE] = (
                f_ref[h].astype(yl_ref.dtype))
    relayout_out(kc, kl)
    for h in range(2):
        yc_ref[0, :, h * LANE:(h + 1) * LANE] = f_ref[h, 0:nc, :].astype(yc_ref.dtype)


def _s5(ul, uc, ops):
    B, n, _ = ul.shape
    nc = uc.shape[1]
    m, pb, pc, adec = ops
    assert n % S5_BLK == 0 and nc % S5_T == 0 and nc // S5_T <= LANE and nc <= S5_BLK
    assert (n // S5_T) % 2 == 0 and (nc // S5_T) % 2 == 0
    ncol = n // S5_T + LANE
    w3 = lambda b: (0, 0, 0)
    return pl.pallas_call(
        _s5_kernel,
        out_shape=(jax.ShapeDtypeStruct((B, n, BRANCH), BF16),
                   jax.ShapeDtypeStruct((B, nc, BRANCH), BF16)),
        grid=(B,),
        in_specs=[pl.BlockSpec((1, n, BRANCH), lambda b: (b, 0, 0)),
                  pl.BlockSpec((1, nc, BRANCH), lambda b: (b, 0, 0)),
                  pl.BlockSpec(m.shape, w3), pl.BlockSpec(pb.shape, w3), pl.BlockSpec(pc.shape, w3),
                  pl.BlockSpec(adec.shape, lambda b: (0, 0))],
        out_specs=(pl.BlockSpec((1, n, BRANCH), lambda b: (b, 0, 0)),
                   pl.BlockSpec((1, nc, BRANCH), lambda b: (b, 0, 0))),
        scratch_shapes=[pltpu.VMEM((S5_G, S5_T * S5_H, ncol), BF16),
                        pltpu.VMEM((ncol, S5_G * S5_P), F32), pltpu.VMEM((ncol, S5_G * S5_P), F32),
                        pltpu.VMEM((ncol, S5_G * S5_P), F32), pltpu.VMEM((ncol, S5_G * S5_P), F32),
                        pltpu.VMEM((2, S5_BLK, LANE), F32)],
        compiler_params=pltpu.CompilerParams(
            dimension_semantics=("parallel",), vmem_limit_bytes=V7X_VMEM_LIMIT),
        name="s5",
    )(ul, uc, m, pb, pc, adec)


POOL_HALO = 16


def _gelu_tanh(x):
    return 0.5 * x * (1.0 + jnp.tanh(math.sqrt(2.0 / math.pi) * (x + 0.044715 * (x * x * x))))


def _merge_kernel(x_ref, att_ref, y5_ref, mix_ref, prev_ref, next_ref, gate_ref,
                  wpool_ref, pscale_ref, wglu_ref, bglu_ref, sg_ref, sb_ref, ws_ref, bs_ref,
                  wout_ref, lng_ref, lnb_ref, o_ref, a_ref, b2_ref, b4_ref, b8_ref,
                  *, n_total, alpha):
    tm = x_ref.shape[1]
    i = pl.program_id(1)
    ni = pl.num_programs(1)
    hl = POOL_HALO

    pool_in = mix_ref[0, :, 0:BRANCH].astype(F32)
    a_ref[0:hl, :] = jnp.where(i > 0, prev_ref[0].astype(F32), 0.0)
    a_ref[hl:hl + tm, :] = pool_in
    a_ref[hl + tm:hl + tm + hl, :] = jnp.where(i < ni - 1, next_ref[0].astype(F32), 0.0)
    r2 = tm + 16
    b2_ref[8:8 + r2, :] = a_ref[7:7 + r2, :] + a_ref[8:8 + r2, :]
    r4 = tm + 12
    b4_ref[10:10 + r4, :] = b2_ref[9:9 + r4, :] + b2_ref[11:11 + r4, :]
    r8 = tm + 8
    b8_ref[12:12 + r8, :] = b4_ref[10:10 + r8, :] + b4_ref[14:14 + r8, :]
    lane = lax.broadcasted_iota(jnp.int32, (CHUNK, BRANCH), 1)
    grp = lane // POOL_GC
    half = jnp.left_shift(1, grp)
    lane_c = lane // SGU_HD

    for c in range(tm // CHUNK):
        r0 = c * CHUNK
        rows = slice(r0, r0 + CHUNK)
        c2 = b2_ref[hl + r0:hl + r0 + CHUNK, :]
        c4 = b4_ref[hl + r0:hl + r0 + CHUNK, :]
        c8 = b8_ref[hl + r0:hl + r0 + CHUNK, :]
        c16 = b8_ref[hl - 4 + r0:hl - 4 + r0 + CHUNK, :] + b8_ref[hl + 4 + r0:hl + 4 + r0 + CHUNK, :]
        tot = jnp.where(grp == 0, c2, jnp.where(grp == 1, c4, jnp.where(grp == 2, c8, c16)))
        t = lax.broadcasted_iota(jnp.int32, (CHUNK, BRANCH), 0) + (i * tm + r0)
        cnt = jnp.minimum(t + half, n_total) - jnp.maximum(t - half, 0)
        pooled = tot / cnt.astype(F32) - a_ref[hl + r0:hl + r0 + CHUNK, :]
        pool_l = _dot(pooled.astype(BF16), wpool_ref[...]) * pscale_ref[...]

        g1 = _gelu_tanh(y5_ref[0, rows, :].astype(F32))
        s5_l = g1 * _sigmoid(_dot(g1.astype(BF16), wglu_ref[...]) + bglu_ref[...])

        su = mix_ref[0, rows, BRANCH:2 * BRANCH].astype(F32)
        sv = mix_ref[0, rows, 2 * BRANCH:3 * BRANCH].astype(F32)
        vc = (_ln_rows(sv) * sg_ref[...] + sb_ref[...]).astype(BF16)
        r = _dot(ws_ref[SGU_HEADS - 1], vc)
        for hd in range(SGU_HEADS - 2, -1, -1):
            r = jnp.where(lane_c == hd, _dot(ws_ref[hd], vc), r)
        sgu_l = su * (r + bs_ref[...])

        gts = mix_ref[0, rows, 3 * BRANCH:].astype(F32)
        att = att_ref[0, :, :, rows].astype(F32).reshape(N_HEADS * V_DIM, CHUNK).T
        cat = jnp.concatenate([att, pool_l, s5_l, sgu_l], axis=-1)
        y = _dot((cat * _silu(gts)).astype(BF16), wout_ref[...])
        z = alpha * x_ref[0, rows, :] + gate_ref[0] * y
        o_ref[0, rows, :] = _ln_rows(z) * lng_ref[...] + lnb_ref[...]


def _merge(x, att, y5, mix, gate, batch_mod, wts, tm, alpha):
    B, n, D = x.shape
    wpool, pscale, wglu, bglu, sg, sb, ws, bs, wout, lng, lnb = wts
    hb = tm // POOL_HALO
    nhb = n // POOL_HALO
    full2 = lambda b, i: (0, 0)
    mod_map = (lambda b, i: (b, 0, 0)) if batch_mod else (lambda b, i: (0, 0, 0))
    row = lambda w: pl.BlockSpec((1, w), full2)
    return pl.pallas_call(
        functools.partial(_merge_kernel, n_total=n, alpha=alpha),
        out_shape=jax.ShapeDtypeStruct((B, n, D), F32),
        grid=(B, n // tm),
        in_specs=[pl.BlockSpec((1, tm, D), lambda b, i: (b, i, 0)),
                  pl.BlockSpec((1, N_HEADS, V_DIM, tm), lambda b, i: (b, 0, 0, i)),
                  pl.BlockSpec((1, tm, BRANCH), lambda b, i: (b, i, 0)),
                  pl.BlockSpec((1, tm, MIX_W), lambda b, i: (b, i, 0)),
                  pl.BlockSpec((1, POOL_HALO, BRANCH),
                               lambda b, i: (b, jnp.maximum(i * hb - 1, 0), 0)),
                  pl.BlockSpec((1, POOL_HALO, BRANCH),
                               lambda b, i: (b, jnp.minimum((i + 1) * hb, nhb - 1), 0)),
                  pl.BlockSpec((1, 1, D), mod_map),
                  pl.BlockSpec((BRANCH, BRANCH), full2), row(BRANCH),
                  pl.BlockSpec((BRANCH, BRANCH), full2), row(BRANCH),
                  row(BRANCH), row(BRANCH),
                  pl.BlockSpec((SGU_HEADS, CHUNK, CHUNK), lambda b, i: (0, 0, 0)),
                  pl.BlockSpec((CHUNK, BRANCH), full2),
                  pl.BlockSpec((4 * BRANCH, D), full2), row(D), row(D)],
        out_specs=pl.BlockSpec((1, tm, D), lambda b, i: (b, i, 0)),
        scratch_shapes=[pltpu.VMEM((tm + 2 * POOL_HALO, BRANCH), F32) for _ in range(4)],
        compiler_params=pltpu.CompilerParams(
            dimension_semantics=("parallel", "parallel"), vmem_limit_bytes=V7X_VMEM_LIMIT),
        name="merge",
    )(x, att, y5, mix, mix, mix, gate, wpool, pscale, wglu, bglu, sg, sb, ws, bs, wout, lng, lnb)


def _prep_inproj_weights(w_in, g_q, w_uq, g_kv, w_ukv):
    D = w_in.shape[0]
    o_kr = KV_LORA
    o_s5 = o_kr + ROPE
    o_cq = o_s5 + BRANCH
    o_rest = o_cq + Q_LORA
    wb = w_in.astype(BF16)
    zpad = lambda k: jnp.zeros((D, k), BF16)
    w = jnp.concatenate(
        [wb[:, 0:KV_LORA], wb[:, o_cq:o_cq + Q_LORA], zpad(C_KR - C_CQ - Q_LORA),
         wb[:, o_kr:o_kr + ROPE], zpad(C_S5 - C_KR - ROPE), wb[:, o_s5:o_s5 + BRANCH],
         wb[:, o_rest:]], axis=1)
    assert w.shape == (D, IN_PAD)
    hw = N_HEADS * HEAD_PAD
    scale = (NOPE + ROPE) ** -0.5 * math.log2(math.e)
    ukv = w_ukv.reshape(KV_LORA, N_HEADS, NOPE + V_DIM)
    uq = w_uq.reshape(Q_LORA, N_HEADS, NOPE + ROPE) * scale
    padl = lambda a: jnp.pad(a, ((0, 0), (0, 0), (0, HEAD_PAD - a.shape[2])))
    wk = padl(ukv[:, :, :NOPE])
    wv = padl(ukv[:, :, NOPE:])
    wq = padl(uq)
    pk = (jnp.arange(LANE)[:, None] + NOPE == jnp.arange(LANE)[None, :]) & (jnp.arange(LANE)[:, None] < ROPE)
    vone = jnp.tile(jnp.arange(HEAD_PAD) == V_DIM, N_HEADS).astype(F32)
    return (w, g_kv.reshape(1, KV_LORA), g_q.reshape(1, Q_LORA),
            wk.reshape(KV_LORA, hw).astype(BF16), wv.reshape(KV_LORA, hw).astype(BF16),
            wq.reshape(Q_LORA, hw).astype(BF16), pk.astype(BF16), vone.reshape(1, hw))


def _rope_tables(n, rotate):
    ones = jnp.ones((n, NOPE), F32)
    zpad = jnp.zeros((n, HEAD_PAD - NOPE - ROPE), F32)
    if not rotate:
        c = jnp.concatenate([ones, jnp.ones((n, ROPE), F32), zpad], axis=1)
        z = jnp.zeros((n, HEAD_PAD), F32)
        return c, z, z
    nf = ROPE // 4
    rows = n // GRID_W
    lane = np.arange(HEAD_PAD)
    blk = (lane - NOPE) // (2 * nf)
    in_rope = (lane >= NOPE) & (lane < NOPE + ROPE)
    first = ((lane - NOPE) % (2 * nf)) < nf
    inv = ROPE_BASE ** (-jnp.arange(nf, dtype=F32) / nf)
    inv_lane = jnp.tile(inv, HEAD_PAD // nf)
    is_row = jnp.asarray(in_rope & (blk == 0))
    is_col = jnp.asarray(in_rope & (blk == 1))
    ang_r = jnp.arange(rows, dtype=F32)[:, None] * inv_lane[None, :]
    ang_c = jnp.arange(GRID_W, dtype=F32)[:, None] * inv_lane[None, :]

    def grid(fr, fc, base):
        t3 = (jnp.where(is_row, fr, 0.0)[:, None, :] + jnp.where(is_col, fc, 0.0)[None, :, :]
              + base[None, None, :])
        return t3.reshape(n, HEAD_PAD)

    nope_one = jnp.asarray((lane < NOPE).astype(np.float32))
    zero = jnp.zeros((HEAD_PAD,), F32)
    m1 = jnp.asarray(first)
    c = grid(jnp.cos(ang_r), jnp.cos(ang_c), nope_one)
    s1 = grid(jnp.where(m1, -jnp.sin(ang_r), 0.0), jnp.where(m1, -jnp.sin(ang_c), 0.0), zero)
    s2 = grid(jnp.where(m1, 0.0, jnp.sin(ang_r)), jnp.where(m1, 0.0, jnp.sin(ang_c)), zero)
    return c, s1, s2


def _s5_selectors():
    T, H = S5_T, S5_H
    nslot = 2 * T - 1
    toep = np.zeros((T, nslot * H, T * H), np.float32)
    for t in range(T):
        for s in range(T):
            slot = 0 if t == s else (t - s if t > s else T - 1 + s - t)
            for h in range(H):
                toep[t, slot * H + h, s * H + h] = 1.0
    e_s = np.kron(np.eye(T, dtype=np.float32), np.ones((1, H), np.float32))
    e_h = np.kron(np.ones((1, T), np.float32), np.eye(H, dtype=np.float32))
    return toep, e_s, e_h


def _prep_s5(lam_re, lam_im, log_dt, b_re, b_im, c_re, c_im, s5_d):
    T, G, P, H = S5_T, S5_G, S5_P, S5_H
    hi = lax.Precision.HIGHEST
    toep, e_s, e_h = _s5_selectors()
    lam_re = lam_re.astype(F32)
    lam_im = lam_im.astype(F32)
    dt = jnp.exp(log_dt.astype(F32))[..., None]
    zr, zi = lam_re * dt, lam_im * dt
    j = jnp.arange(T + 1, dtype=F32)[None, None, :, None]
    mag = jnp.exp(zr[:, :, None, :] * j)
    ang = zi[:, :, None, :] * j
    pr, pi = mag * jnp.cos(ang), mag * jnp.sin(ang)
    nr, ni = pr[:, :, 1] - 1.0, pi[:, :, 1]
    den = lam_re * lam_re + lam_im * lam_im
    k_re = (nr * lam_re + ni * lam_im) / den
    k_im = (ni * lam_re - nr * lam_im) / den
    bb_re = k_re[..., None] * b_re - k_im[..., None] * b_im
    bb_im = k_re[..., None] * b_im + k_im[..., None] * b_re
    cre, cim = c_re[:, :, None, :, :], c_im[:, :, None, :, :]
    prb, pib = pr[:, :, :, None, :], pi[:, :, :, None, :]
    cl_re = cre * prb - cim * pib
    cl_im = cre * pib + cim * prb
    kk = (jnp.einsum('dgjop,dgph->dgjoh', cl_re[:, :, :T], bb_re, precision=hi)
          - jnp.einsum('dgjop,dgph->dgjoh', cl_im[:, :, :T], bb_im, precision=hi))
    skip = jnp.eye(H, dtype=F32)[None] * s5_d.astype(F32).reshape(G, H, 1)
    x = jnp.concatenate([(kk[0, :, 0] + kk[1, :, 0] + skip)[:, None], kk[0, :, 1:], kk[1, :, 1:]],
                        axis=1)
    x = x.transpose(0, 2, 1, 3).reshape(G, H, (2 * T - 1) * H)
    m = jnp.einsum('goJ,tJc->gtoc', x.astype(BF16), jnp.asarray(toep, BF16),
                   preferred_element_type=F32).reshape(G, T * H, T * H)
    over_s = lambda a: jnp.einsum('gap,ac->gpc', a, jnp.asarray(e_s), precision=hi)
    over_h = lambda a: jnp.einsum('gpa,ac->gpc', a, jnp.asarray(e_h), precision=hi)

    def lb(p_r, p_i, d):
        ar, ai, br, bi = over_s(p_r), over_s(p_i), over_h(bb_re[d]), over_h(bb_im[d])
        return ar * br - ai * bi, ar * bi + ai * br

    f_re, f_im = lb(pr[0, :, T - 1::-1], pi[0, :, T - 1::-1], 0)
    b_re_, b_im_ = lb(pr[1, :, :T], pi[1, :, :T], 1)
    pb = jnp.concatenate([f_re, b_re_, f_im, b_im_], axis=1)
    to_c = lambda a: a.reshape(G, T * H, P)
    zf_re, zf_im = cl_re[0, :, 1:], cl_im[0, :, 1:]
    zb_re, zb_im = cl_re[1, :, T:0:-1], cl_im[1, :, T:0:-1]
    pc = jnp.concatenate([to_c(zf_re), to_c(zb_re), -to_c(zf_im), -to_c(zb_im)], axis=2)
    adec = jnp.stack([pr[0, :, T], pi[0, :, T], pr[1, :, T], pi[1, :, T]], axis=0).reshape(4, G * P)
    return (m.astype(BF16), pb.astype(BF16), pc.astype(BF16), adec)


def _prep_merge_weights(w_pool, pool_scale, w_glu, b_glu, sgu_g, sgu_b, w_s, b_s, w_out, ln_g, ln_b):
    z = jnp.zeros((POOL_GC, POOL_GC), F32)
    ng = len(POOL_WINDOWS)
    wpool = jnp.concatenate(
        [jnp.concatenate([w_pool[gi] if gj == gi else z for gj in range(ng)], axis=1)
         for gi in range(ng)], axis=0)
    bs = jnp.repeat(b_s.T, SGU_HD, axis=1)
    r = lambda a: a.reshape(1, -1).astype(F32)
    return (wpool.astype(BF16), r(pool_scale), w_glu.astype(BF16), r(b_glu), r(sgu_g), r(sgu_b),
            w_s.astype(BF16), bs.astype(F32), w_out.astype(BF16), r(ln_g), r(ln_b))


def kernel(x, c, ctx, c_ctx, w_mod, b_mod, w_in, g_q, w_uq, g_kv, w_ukv, w_pool, pool_scale, lam_re, lam_im, log_dt, s5_b_re, s5_b_im, s5_c_re, s5_c_im, s5_d, w_glu, b_glu, sgu_g, sgu_b, w_s, b_s, w_out, ln_g, ln_b):
    B, n, D = x.shape
    nc = ctx.shape[1]
    depth = w_mod.shape[0]
    alpha = (2 * depth) ** 0.25
    assert B + 1 <= 8
    tm = min(ROW_TILE, n)
    tmc = min(ROW_TILE, nc)
    tq = min(512, n)
    tqc = min(512, nc)

    cc = jnp.concatenate([c.astype(F32), c_ctx.astype(F32).reshape(1, D),
                          jnp.zeros((8 - B - 1, D), F32)], axis=0)
    mod = _modulation(cc, w_mod, b_mod.reshape(depth, 1, 3 * D))
    tab_l = _rope_tables(n, True)
    tab_c = _rope_tables(nc, False)

    wts_in_all = jax.vmap(_prep_inproj_weights)(w_in, g_q, w_uq, g_kv, w_ukv)
    s5_ops_all = jax.vmap(_prep_s5)(lam_re, lam_im, log_dt, s5_b_re, s5_b_im, s5_c_re, s5_c_im, s5_d)
    wts_mg_all = jax.vmap(_prep_merge_weights)(w_pool, pool_scale, w_glu, b_glu, sgu_g, sgu_b,
                                               w_s, b_s, w_out, ln_g, ln_b)

    for l in range(depth):
        last = l == depth - 1
        shift = mod[l, :, 0:D].reshape(8, 1, D)
        scale = mod[l, :, D:2 * D].reshape(8, 1, D)
        gate = mod[l, :, 2 * D:].reshape(8, 1, D)
        wts_in = tuple(a[l] for a in wts_in_all)
        s5_ops = tuple(a[l] for a in s5_ops_all)
        wts_mg = tuple(a[l] for a in wts_mg_all)

        q_l, k_l, v_l, s5u_l, mix_l = _inproj(x, shift[:B], scale[:B], True, wts_in, tab_l, tm)
        q_c, k_c, v_c, s5u_c, mix_c = _inproj(ctx, shift[B:B + 1], scale[B:B + 1], False, wts_in, tab_c, tmc)
        att_l = _attention(q_l, [k_l, k_c], [v_l, v_c], tq)
        y5_l, y5_c = _s5(s5u_l, s5u_c, s5_ops)
        x_new = _merge(x, att_l, y5_l, mix_l, gate[:B], True, wts_mg, tm, alpha)
        if not last:
            att_c = _attention(q_c, [k_c], [v_c], tqc)
            ctx = _merge(ctx, att_c, y5_c, mix_c, gate[B:B + 1], False, wts_mg, tmc, alpha)
        x = x_new
    return x
```
